```python
import math
import jax, jax.numpy as jnp
from jax import lax
import numpy as np

D_MODEL = 4096
BATCH = 4
SEQ = 4096
DEPTH = 4

N_MIXERS = 2
N_CONV_LAYERS = (DEPTH + 1) // 2
N_RWKV_LAYERS = DEPTH // 2
N_VRES_LAYERS = max(N_RWKV_LAYERS - 1, 0)
CONV_WIDTH = 3
D_FF = 4 * D_MODEL
HEAD_SIZE = 64
N_HEADS = D_MODEL // HEAD_SIZE
D_DECAY_LORA = max(32, int(round(math.sqrt(D_MODEL) * 1.8 / 32)) * 32)
D_AAA_LORA = max(32, int(round(math.sqrt(D_MODEL) * 1.8 / 32)) * 32)
D_MV_LORA = max(32, int(round(math.sqrt(D_MODEL) * 1.3 / 32)) * 32)
D_GATE_LORA = max(32, int(round(D_MODEL ** 0.8 * 0.6 / 32)) * 32)
ADA_CHUNKS = 6
RMS_EPS = 1e-5
GN_EPS = 64e-5

kernel_name = "hybrid_conv_rwkv7_adaln_trunk"


def rms_norm(x, g):
    xf = x.astype(jnp.float32)
    y = xf * lax.rsqrt(jnp.mean(xf * xf, axis=-1, keepdims=True) + RMS_EPS)
    return (y * g.astype(jnp.float32)).astype(x.dtype)


def token_shift(x):
    return jnp.pad(x[:, :-1], ((0, 0), (1, 0), (0, 0)))


def short_conv_mixer(h, w_in, conv_w, w_out):
    b_gate, c_gate, u = jnp.split(h @ w_in, 3, axis=-1)
    z = c_gate * u
    T = h.shape[1]
    zp = jnp.pad(z, ((0, 0), (CONV_WIDTH - 1, 0), (0, 0)))
    y = sum(conv_w[j] * zp[:, j:j + T] for j in range(CONV_WIDTH))
    return (b_gate * y) @ w_out


def sqrelu_mlp(h, w1, w2):
    return jnp.square(jax.nn.relu(h @ w1)) @ w2


def wkv7_scan(r, decay, k, v, kk, a):
    def step(S, inp):
        r_t, w_t, k_t, v_t, kk_t, a_t = inp
        sa = jnp.einsum("bhvk,bhk->bhv", S, -kk_t)
        S = (S * w_t[:, :, None, :]
             + sa[..., None] * (kk_t * a_t)[:, :, None, :]
             + v_t[..., None] * k_t[:, :, None, :])
        return S, jnp.einsum("bhvk,bhk->bhv", S, r_t)

    bsz, _, n_h, n = r.shape
    xs = tuple(jnp.moveaxis(t, 1, 0) for t in (r, decay, k, v, kk, a))
    s0 = jnp.zeros((bsz, n_h, n, n), jnp.float32)
    _, y = lax.scan(step, s0, xs)
    return jnp.moveaxis(y, 0, 1)


def rwkv7_time_mix(h, mu, w_rkv, w0, w1, w2, a0, a1, a2, g1, g2, k_k, k_a, r_k,
                   lnx_g, lnx_b, w_o, v_first=None, v0=None, v1=None, v2=None):
    bsz, T, D = h.shape
    xx = token_shift(h) - h
    mixes = h[:, :, None, :] + xx[:, :, None, :] * mu
    rkv = jnp.einsum("btjd,jde->btje", mixes[:, :, :3], w_rkv)
    r, k, v = rkv[:, :, 0], rkv[:, :, 1], rkv[:, :, 2]
    xv, xw, xa, xg = mixes[:, :, 2], mixes[:, :, 3], mixes[:, :, 4], mixes[:, :, 5]
    w_log = -jax.nn.softplus(-(w0 + jnp.tanh(xw @ w1) @ w2)) - 0.5
    decay = jnp.exp(-jnp.exp(w_log.astype(jnp.float32)))
    a = jax.nn.sigmoid(a0 + (xa @ a1) @ a2)
    g = jax.nn.sigmoid(xg @ g1) @ g2
    if v_first is not None:
        v = v + (v_first - v) * jax.nn.sigmoid(v0 + (xv @ v1) @ v2)

    def heads(t):
        return t.reshape(bsz, T, N_HEADS, HEAD_SIZE).astype(jnp.float32)

    kk = heads(k * k_k)
    kk = kk * lax.rsqrt(jnp.maximum(jnp.sum(kk * kk, axis=-1, keepdims=True), 1e-24))
    k = k * (1.0 + (a - 1.0) * k_a)
    rh, kh, vh = heads(r), heads(k), heads(v)
    y = wkv7_scan(rh, heads(decay), kh, vh, kk, heads(a))
    mean = jnp.mean(y, axis=-1, keepdims=True)
    var = jnp.mean(jnp.square(y - mean), axis=-1, keepdims=True)
    y = ((y - mean) * lax.rsqrt(var + GN_EPS)).reshape(bsz, T, D) * lnx_g + lnx_b
    bonus = jnp.sum(rh * kh * r_k, axis=-1, keepdims=True) * vh
    y = y + bonus.reshape(bsz, T, D)
    out = (y.astype(h.dtype) * g) @ w_o
    return out, v


def setup_inputs(seed: int = 0) -> dict:
    key = jax.random.key(seed)
    ks = iter(jax.random.split(key, 48))
    D = D_MODEL

    def nrm(shape, scale):
        return scale * jax.random.normal(next(ks), shape, jnp.float32)

    def unif(shape, lo, hi):
        return jax.random.uniform(next(ks), shape, jnp.float32, lo, hi)

    R = N_RWKV_LAYERS
    return {
        "x": nrm((BATCH, SEQ, D), 1.0),
        "c": nrm((BATCH, D), 1.0),
        "ada_w": nrm((D, ADA_CHUNKS * D), 0.5 * D ** -0.5),
        "ada_b": nrm((ADA_CHUNKS * D,), 0.02),
        "ada_emb": nrm((DEPTH, ADA_CHUNKS, D), 0.1),
        "ln1_g": 1.0 + nrm((DEPTH, D), 0.05),
        "ln2_g": 1.0 + nrm((DEPTH, D), 0.05),
        "mlp_w1": nrm((DEPTH, D, D_FF), D ** -0.5),
        "mlp_w2": nrm((DEPTH, D_FF, D), D_FF ** -0.5),
        "conv_w_in": nrm((N_CONV_LAYERS, D, 3 * D), D ** -0.5),
        "conv_w": nrm((N_CONV_LAYERS, CONV_WIDTH, D), CONV_WIDTH ** -0.5),
        "conv_w_out": nrm((N_CONV_LAYERS, D, D), D ** -0.5),
        "rwkv_mu": unif((R, 6, D), 0.0, 1.0),
        "rwkv_w_rkv": nrm((R, 3, D, D), D ** -0.5),
        "rwkv_w0": unif((R, D), -6.0, -1.0),
        "rwkv_w1": nrm((R, D, D_DECAY_LORA), 0.1 * D ** -0.5),
        "rwkv_w2": nrm((R, D_DECAY_LORA, D), 0.1 * D_DECAY_LORA ** -0.5),
        "rwkv_a0": nrm((R, D), 0.5),
        "rwkv_a1": nrm((R, D, D_AAA_LORA), 0.5 * D ** -0.5),
        "rwkv_a2": nrm((R, D_AAA_LORA, D), 0.5 * D_AAA_LORA ** -0.5),
        "rwkv_g1": nrm((R, D, D_GATE_LORA), D ** -0.5),
        "rwkv_g2": nrm((R, D_GATE_LORA, D), D_GATE_LORA ** -0.5),
        "rwkv_k_k": 0.85 + nrm((R, D), 0.05),
        "rwkv_k_a": 1.0 + nrm((R, D), 0.05),
        "rwkv_r_k": nrm((R, N_HEADS, HEAD_SIZE), 0.1),
        "rwkv_lnx_g": 1.0 + nrm((R, D), 0.05),
        "rwkv_lnx_b": nrm((R, D), 0.01),
        "rwkv_w_o": nrm((R, D, D), D ** -0.5),
        "rwkv_v0": 1.0 + nrm((N_VRES_LAYERS, D), 0.3),
        "rwkv_v1": nrm((N_VRES_LAYERS, D, D_MV_LORA), 0.5 * D ** -0.5),
        "rwkv_v2": nrm((N_VRES_LAYERS, D_MV_LORA, D), 0.5 * D_MV_LORA ** -0.5),
        "final_g": 1.0 + nrm((D,), 0.05),
    }


def reference(x, c, ada_w, ada_b, ada_emb, ln1_g, ln2_g, mlp_w1, mlp_w2,
              conv_w_in, conv_w, conv_w_out,
              rwkv_mu, rwkv_w_rkv, rwkv_w0, rwkv_w1, rwkv_w2, rwkv_a0, rwkv_a1, rwkv_a2,
              rwkv_g1, rwkv_g2, rwkv_k_k, rwkv_k_a, rwkv_r_k, rwkv_lnx_g, rwkv_lnx_b, rwkv_w_o,
              rwkv_v0, rwkv_v1, rwkv_v2, final_g):
    bsz, _, D = x.shape
    mod_shared = (jax.nn.silu(c) @ ada_w + ada_b).reshape(bsz, ADA_CHUNKS, D)
    v_first = None
    for i in range(DEPTH):
        mod = mod_shared + ada_emb[i][None]
        shift1, scale1, gate1, shift2, scale2, gate2 = [mod[:, j, None, :] for j in range(ADA_CHUNKS)]
        h = rms_norm(x, ln1_g[i]) * (1.0 + scale1) + shift1
        j = i // N_MIXERS
        if i % N_MIXERS == 0:
            y = short_conv_mixer(h, conv_w_in[j], conv_w[j], conv_w_out[j])
        else:
            args = (rwkv_mu[j], rwkv_w_rkv[j], rwkv_w0[j], rwkv_w1[j], rwkv_w2[j],
                    rwkv_a0[j], rwkv_a1[j], rwkv_a2[j], rwkv_g1[j], rwkv_g2[j],
                    rwkv_k_k[j], rwkv_k_a[j], rwkv_r_k[j], rwkv_lnx_g[j], rwkv_lnx_b[j], rwkv_w_o[j])
            if v_first is None:
                y, v_first = rwkv7_time_mix(h, *args)
            else:
                y, _ = rwkv7_time_mix(h, *args, v_first=v_first,
                                      v0=rwkv_v0[j - 1], v1=rwkv_v1[j - 1], v2=rwkv_v2[j - 1])
        x = x + gate1 * y
        h = rms_norm(x, ln2_g[i]) * (1.0 + scale2) + shift2
        x = x + gate2 * sqrelu_mlp(h, mlp_w1[i], mlp_w2[i])
    return rms_norm(x, final_g)
```

```python
import functools

import jax
import jax.numpy as jnp
from jax import lax
from jax.experimental import pallas as pl
from jax.experimental.pallas import tpu as pltpu

F32 = jnp.float32
BF16 = jnp.bfloat16

HEAD = 64
CHUNK = 64
SUB = 16
GROUP_HEADS = 4
GROUP = GROUP_HEADS * HEAD
LANE = 128
SUBLANE = 8
ADA_CHUNKS = 6
CONV_WIDTH = 3
RMS_EPS = 1e-5
GN_EPS = 64e-5
VMEM_LIMIT = 56 * 1024 * 1024


def _blk(n, pref, align):
    best = None
    d = align
    while d <= min(n, pref):
        if n % d == 0:
            best = d
        d += align
    return best if best is not None else n


def _cparams(sem):
    return pltpu.CompilerParams(dimension_semantics=sem, vmem_limit_bytes=VMEM_LIMIT)


def _dot(a, b):
    return jnp.dot(a, b, preferred_element_type=F32)


def _dot_nt(a, b):
    return lax.dot_general(a, b, (((1,), (1,)), ((), ())), preferred_element_type=F32)


def _dot_tn(a, b):
    return lax.dot_general(a, b, (((0,), (0,)), ((), ())), preferred_element_type=F32)


def _split3(x):
    h1 = x.astype(BF16)
    r1 = x - h1.astype(F32)
    h2 = r1.astype(BF16)
    h3 = (r1 - h2.astype(F32)).astype(BF16)
    return h1, h2, h3


def _head_ones(n):
    r = lax.broadcasted_iota(jnp.int32, (n, n), 0) // HEAD
    c = lax.broadcasted_iota(jnp.int32, (n, n), 1) // HEAD
    return jnp.where(r == c, 1.0, 0.0).astype(BF16)


def _segsum(x, ones):
    outs = []
    for s in range(x.shape[1] // LANE):
        h1, h2, h3 = _split3(x[:, s * LANE:(s + 1) * LANE])
        outs.append(_dot(h1, ones) + _dot(h2, ones) + _dot(h3, ones))
    return outs[0] if len(outs) == 1 else jnp.concatenate(outs, axis=1)


def _ada_kernel(c_ref, w_ref, b_ref, emb_ref, o_ref, *, depth):
    s = jax.nn.silu(c_ref[...])
    acc = jnp.dot(s, w_ref[...], preferred_element_type=F32,
                  precision=lax.Precision.HIGHEST) + b_ref[...]
    for i in range(depth):
        o_ref[i] = acc + emb_ref[i]


def _ada(c, ada_w, ada_b, ada_emb):
    bsz, d = c.shape
    depth = ada_emb.shape[0]
    n = ada_w.shape[1]
    rows = -(-bsz // SUBLANE) * SUBLANE
    cp = jnp.pad(c, ((0, rows - bsz), (0, 0)))
    bn = _blk(n, 512, LANE)
    out = pl.pallas_call(
        functools.partial(_ada_kernel, depth=depth),
        grid=(n // bn,),
        in_specs=[
            pl.BlockSpec((rows, d), lambda j: (0, 0)),
            pl.BlockSpec((d, bn), lambda j: (0, j)),
            pl.BlockSpec((1, bn), lambda j: (0, j)),
            pl.BlockSpec((depth, 1, bn), lambda j: (0, 0, j)),
        ],
        out_specs=pl.BlockSpec((depth, rows, bn), lambda j: (0, 0, j)),
        out_shape=jax.ShapeDtypeStruct((depth, rows, n), F32),
        compiler_params=_cparams(("parallel",)),
        name="ada_mod",
    )(cp, ada_w, ada_b.reshape(1, n), ada_emb.reshape(depth, 1, n))
    mod = out[:, :bsz].reshape(depth, bsz, ADA_CHUNKS, d)
    return jnp.transpose(mod, (0, 2, 1, 3)).reshape(depth, ADA_CHUNKS, bsz, 1, d)


def _mod_spec(layer, chunk, tpb, bn, col_axis):
    if col_axis is None:
        return pl.BlockSpec((None, None, None, 1, bn), lambda i: (layer, chunk, i // tpb, 0, 0))
    if col_axis == 1:
        return pl.BlockSpec((None, None, None, 1, bn),
                            lambda i, j, *_: (layer, chunk, i // tpb, 0, j))
    return pl.BlockSpec((None, None, None, 1, bn),
                        lambda j, i: (layer, chunk, i // tpb, 0, j))


def _rms(x, g):
    ms = jnp.mean(x * x, axis=-1, keepdims=True)
    return (x * lax.rsqrt(ms + RMS_EPS)) * g


def _norm_mod_kernel(x_ref, g_ref, sc_ref, sh_ref, o_ref):
    y = _rms(x_ref[...], g_ref[...])
    o_ref[...] = (y * (1.0 + sc_ref[...]) + sh_ref[...]).astype(o_ref.dtype)


def _norm_mod(x, g, mod, layer, sc_chunk, sh_chunk, seq):
    n, d = x.shape
    bm = _blk(seq, 256, SUBLANE)
    tpb = seq // bm
    return pl.pallas_call(
        _norm_mod_kernel,
        grid=(n // bm,),
        in_specs=[
            pl.BlockSpec((bm, d), lambda i: (i, 0)),
            pl.BlockSpec((1, d), lambda i: (0, 0)),
            _mod_spec(layer, sc_chunk, tpb, d, None),
            _mod_spec(layer, sh_chunk, tpb, d, None),
        ],
        out_specs=pl.BlockSpec((bm, d), lambda i: (i, 0)),
        out_shape=jax.ShapeDtypeStruct((n, d), BF16),
        compiler_params=_cparams(("parallel",)),
        name="norm_mod",
    )(x, g.reshape(1, d), mod, mod)


def _final_norm_kernel(x_ref, g_ref, o_ref):
    o_ref[...] = _rms(x_ref[...], g_ref[...])


def _final_norm(x, g):
    n, d = x.shape
    bm = _blk(n, 512, SUBLANE)
    return pl.pallas_call(
        _final_norm_kernel,
        grid=(n // bm,),
        in_specs=[pl.BlockSpec((bm, d), lambda i: (i, 0)),
                  pl.BlockSpec((1, d), lambda i: (0, 0))],
        out_specs=pl.BlockSpec((bm, d), lambda i: (i, 0)),
        out_shape=jax.ShapeDtypeStruct((n, d), F32),
        compiler_params=_cparams(("parallel",)),
        name="final_norm",
    )(x, g.reshape(1, d))


def _mix_kernel(x_ref, xp_ref, g_ref, sc_ref, sh_ref, mu_ref, wd_ref, mix_ref, hid_ref,
                *, tpb, widths, use_v):
    i = pl.program_id(0)

    def nm(x):
        return _rms(x, g_ref[...]) * (1.0 + sc_ref[...]) + sh_ref[...]

    h = nm(x_ref[...])
    hp = nm(xp_ref[...])[SUBLANE - 1:SUBLANE]
    hp = jnp.where(i % tpb == 0, 0.0, hp)
    rows = lax.broadcasted_iota(jnp.int32, h.shape, 0)
    xx = jnp.where(rows == 0, hp, pltpu.roll(h, 1, 0)) - h

    def mix(j):
        return (h + xx * mu_ref[j:j + 1, :]).astype(BF16)

    mix_ref[0] = mix(0)
    mix_ref[1] = mix(1)
    xv = mix(2)
    mix_ref[2] = xv
    wl, al, gl, vl = widths
    o = 0
    hid_ref[:, o:o + wl] = jnp.tanh(_dot(mix(3), wd_ref[:, o:o + wl])).astype(BF16)
    o += wl
    hid_ref[:, o:o + al] = _dot(mix(4), wd_ref[:, o:o + al]).astype(BF16)
    o += al
    hid_ref[:, o:o + gl] = jax.nn.sigmoid(_dot(mix(5), wd_ref[:, o:o + gl])).astype(BF16)
    o += gl
    if use_v:
        hid_ref[:, o:o + vl] = _dot(xv, wd_ref[:, o:o + vl]).astype(BF16)


def _mix(x, g, mod, layer, mu, wd, widths, use_v, seq):
    n, d = x.shape
    hl = wd.shape[1]
    bm = _blk(seq, 128, SUBLANE)
    tpb = seq // bm
    per8 = bm // SUBLANE
    return pl.pallas_call(
        functools.partial(_mix_kernel, tpb=tpb, widths=widths, use_v=use_v),
        grid=(n // bm,),
        in_specs=[
            pl.BlockSpec((bm, d), lambda i: (i, 0)),
            pl.BlockSpec((SUBLANE, d), lambda i: (jnp.maximum(i * per8 - 1, 0), 0)),
            pl.BlockSpec((1, d), lambda i: (0, 0)),
            _mod_spec(layer, 1, tpb, d, None),
            _mod_spec(layer, 0, tpb, d, None),
            pl.BlockSpec((6, d), lambda i: (0, 0)),
            pl.BlockSpec((d, hl), lambda i: (0, 0)),
        ],
        out_specs=[pl.BlockSpec((3, bm, d), lambda i: (0, i, 0)),
                   pl.BlockSpec((bm, hl), lambda i: (i, 0))],
        out_shape=[jax.ShapeDtypeStruct((3, n, d), BF16),
                   jax.ShapeDtypeStruct((n, hl), BF16)],
        compiler_params=_cparams(("parallel",)),
        name="rwkv_mix",
    )(x, x, g.reshape(1, d), mod, mod, mu, wd)


def _conv_in_kernel(h_ref, wb_ref, wc_ref, wu_ref, cw_ref, o_ref, carry_ref, *, tpb):
    m = pl.program_id(1)
    h = h_ref[...]
    bg = _dot(h, wb_ref[...])
    z = _dot(h, wc_ref[...]) * _dot(h, wu_ref[...])
    bm = z.shape[0]
    prev = jnp.where(m % tpb == 0, 0.0, carry_ref[...])
    carry_ref[...] = z[bm - SUBLANE:bm]
    w0 = cw_ref[0:1, :]
    w1 = cw_ref[1:2, :]
    w2 = cw_ref[2:3, :]
    y = w0 * pltpu.roll(z, 2, 0) + w1 * pltpu.roll(z, 1, 0) + w2 * z
    o_ref[...] = (bg * y).astype(o_ref.dtype)
    zt = z[0:SUBLANE]
    r8 = lax.broadcasted_iota(jnp.int32, zt.shape, 0)
    z1 = jnp.where(r8 < 1, pltpu.roll(prev, 1, 0), pltpu.roll(zt, 1, 0))
    z2 = jnp.where(r8 < 2, pltpu.roll(prev, 2, 0), pltpu.roll(zt, 2, 0))
    yt = w0 * z2 + w1 * z1 + w2 * zt
    o_ref[0:SUBLANE, :] = (bg[0:SUBLANE] * yt).astype(o_ref.dtype)


def _conv_in(h, w_in, conv_w, seq):
    n, d = h.shape
    bm = _blk(seq, 1024, SUBLANE)
    bn = _blk(d, 256, LANE)
    tpb = seq // bm
    nj = d // bn
    return pl.pallas_call(
        functools.partial(_conv_in_kernel, tpb=tpb),
        grid=(nj, n // bm),
        in_specs=[
            pl.BlockSpec((bm, d), lambda j, i: (i, 0)),
            pl.BlockSpec((d, bn), lambda j, i: (0, j)),
            pl.BlockSpec((d, bn), lambda j, i: (0, j + nj)),
            pl.BlockSpec((d, bn), lambda j, i: (0, j + 2 * nj)),
            pl.BlockSpec((CONV_WIDTH, bn), lambda j, i: (0, j)),
        ],
        out_specs=pl.BlockSpec((bm, bn), lambda j, i: (i, j)),
        out_shape=jax.ShapeDtypeStruct((n, d), BF16),
        scratch_shapes=[pltpu.VMEM((SUBLANE, bn), F32)],
        compiler_params=_cparams(("arbitrary", "arbitrary")),
        name="conv_in",
    )(h, w_in, w_in, w_in, conv_w)


def _mm_resid_kernel(a_ref, w_ref, x_ref, gate_ref, o_ref, *, nk):
    part = _dot(a_ref[...], w_ref[...])
    if nk == 1:
        o_ref[...] = x_ref[...] + gate_ref[...] * part
        return
    k = pl.program_id(2)

    @pl.when(k == 0)
    def _():
        o_ref[...] = part

    @pl.when(jnp.logical_and(k > 0, k < nk - 1))
    def _():
        o_ref[...] += part

    @pl.when(k == nk - 1)
    def _():
        o_ref[...] = x_ref[...] + gate_ref[...] * (o_ref[...] + part)


def _mm_resid(a, w, x, mod, layer, gate_chunk, seq):
    n, kdim = a.shape
    d = w.shape[1]
    bm = _blk(seq, 1024, SUBLANE)
    bk = _blk(kdim, 4096 if kdim <= 4096 else 2048, LANE)
    nk = kdim // bk
    bn = _blk(d, 512 if nk == 1 else 1024, LANE)
    tpb = seq // bm
    return pl.pallas_call(
        functools.partial(_mm_resid_kernel, nk=nk),
        grid=(n // bm, d // bn, nk),
        in_specs=[
            pl.BlockSpec((bm, bk), lambda i, j, k: (i, k)),
            pl.BlockSpec((bk, bn), lambda i, j, k: (k, j)),
            pl.BlockSpec((bm, bn), lambda i, j, k: (i, j)),
            _mod_spec(layer, gate_chunk, tpb, bn, 1),
        ],
        out_specs=pl.BlockSpec((bm, bn), lambda i, j, k: (i, j)),
        out_shape=jax.ShapeDtypeStruct((n, d), F32),
        compiler_params=_cparams(("parallel", "parallel", "arbitrary")),
        name="mm_resid",
    )(a, w, x, mod)


def _mlp_up_kernel(a_ref, w_ref, o_ref):
    o_ref[...] = jnp.square(jnp.maximum(_dot(a_ref[...], w_ref[...]), 0.0)).astype(o_ref.dtype)


def _mlp_up(a, w):
    n, kdim = a.shape
    f = w.shape[1]
    bm = _blk(n, 1024, SUBLANE)
    bn = _blk(f, 1024, LANE)
    return pl.pallas_call(
        _mlp_up_kernel,
        grid=(n // bm, f // bn),
        in_specs=[pl.BlockSpec((bm, kdim), lambda i, j: (i, 0)),
                  pl.BlockSpec((kdim, bn), lambda i, j: (0, j))],
        out_specs=pl.BlockSpec((bm, bn), lambda i, j: (i, j)),
        out_shape=jax.ShapeDtypeStruct((n, f), BF16),
        compiler_params=_cparams(("parallel", "parallel")),
        name="mlp_up",
    )(a, w)


def _rkv_kernel(a_ref, w_ref, o_ref):
    o_ref[...] = _dot(a_ref[...], w_ref[...])


def _rkv(mixes, w_rkv):
    _, n, d = mixes.shape
    bm = _blk(n, 1024, SUBLANE)
    bn = _blk(d, 1024, LANE)
    return pl.pallas_call(
        _rkv_kernel,
        grid=(3, n // bm, d // bn),
        in_specs=[pl.BlockSpec((None, bm, d), lambda s, i, j: (s, i, 0)),
                  pl.BlockSpec((None, d, bn), lambda s, i, j: (s, 0, j))],
        out_specs=pl.BlockSpec((None, bm, bn), lambda s, i, j: (s, i, j)),
        out_shape=jax.ShapeDtypeStruct((3, n, d), F32),
        compiler_params=_cparams(("parallel", "parallel", "parallel")),
        name="rkv_proj",
    )(mixes, w_rkv)


def _prep_kernel(*refs, widths, use_v):
    if use_v:
        (hid_ref, wu_ref, k_ref, v_ref, vf_ref, w0_ref, a0_ref, kk_ref, ka_ref, v0_ref,
         lw_ref, a_ref, g_ref, kkn_ref, km_ref, vm_ref) = refs
    else:
        (hid_ref, wu_ref, k_ref, w0_ref, a0_ref, kk_ref, ka_ref,
         lw_ref, a_ref, g_ref, kkn_ref, km_ref) = refs
    wl, al, gl, vl = widths
    o = 0
    x = w0_ref[...] + _dot(hid_ref[:, o:o + wl], wu_ref[o:o + wl, :])
    o += wl
    sp = jnp.maximum(-x, 0.0) + jnp.log(1.0 + jnp.exp(-jnp.abs(x)))
    lw_ref[...] = -jnp.exp(-sp - 0.5)
    a = jax.nn.sigmoid(a0_ref[...] + _dot(hid_ref[:, o:o + al], wu_ref[o:o + al, :]))
    o += al
    a_ref[...] = a
    g_ref[...] = _dot(hid_ref[:, o:o + gl], wu_ref[o:o + gl, :])
    o += gl
    k = k_ref[...]
    kk = k * kk_ref[...]
    ss = _segsum(kk * kk, _head_ones(LANE))
    kkn_ref[...] = kk * lax.rsqrt(jnp.maximum(ss, 1e-24))
    km_ref[...] = k * (1.0 + (a - 1.0) * ka_ref[...])
    if use_v:
        v = v_ref[...]
        vg = jax.nn.sigmoid(v0_ref[...] + _dot(hid_ref[:, o:o + vl], wu_ref[o:o + vl, :]))
        vm_ref[...] = v + (vf_ref[...] - v) * vg


def _prep(hid, wu, rkv, v_first, w0, a0, k_k, k_a, v0, widths, use_v):
    n, hl = hid.shape
    d = wu.shape[1]
    bm = _blk(n, 512, SUBLANE)
    bn = _blk(d, 512, LANE)
    row = lambda p: p.reshape(1, d)
    blk = pl.BlockSpec((bm, bn), lambda i, j: (i, j))
    prow = pl.BlockSpec((1, bn), lambda i, j: (0, j))
    ins = [hid, wu, rkv]
    specs = [pl.BlockSpec((bm, hl), lambda i, j: (i, 0)),
             pl.BlockSpec((hl, bn), lambda i, j: (0, j)),
             pl.BlockSpec((None, bm, bn), lambda i, j: (1, i, j))]
    if use_v:
        ins += [rkv, v_first]
        specs += [pl.BlockSpec((None, bm, bn), lambda i, j: (2, i, j)), blk]
    ins += [row(w0), row(a0), row(k_k), row(k_a)]
    specs += [prow] * 4
    n_out = 5
    if use_v:
        ins.append(row(v0))
        specs.append(prow)
        n_out = 6
    return pl.pallas_call(
        functools.partial(_prep_kernel, widths=widths, use_v=use_v),
        grid=(n // bm, d // bn),
        in_specs=specs,
        out_specs=[blk] * n_out,
        out_shape=[jax.ShapeDtypeStruct((n, d), F32)] * n_out,
        compiler_params=_cparams(("parallel", "parallel")),
        name="rwkv_prep",
    )(*ins)


def _wkv_group(s_prev, r, lw, k, v, kk, a, consts):
    tri, head_masks, strict, incl, blk16, eye = consts
    h1, h2, h3 = _split3(lw)
    cum = _dot(tri, h1) + _dot(tri, h2) + _dot(tri, h3)
    cl = cum[CHUNK - 1:CHUNK]
    gi = jnp.exp(-cum)
    b = kk * a
    rt = r * jnp.exp(cum)
    at = -(kk * jnp.exp(cum - lw))
    kh = k * gi
    bh = b * gi
    dl = jnp.exp(cl - cum)
    kend = k * dl
    bend = b * dl

    def stack(x):
        return jnp.concatenate([jnp.where(m, x, 0.0) for m in head_masks], axis=0).astype(BF16)

    at_s, rt_s, kh_s, bh_s = stack(at), stack(rt), stack(kh), stack(bh)
    v_s, kend_s, bend_s = stack(v), stack(kend), stack(bend)
    ar = jnp.concatenate([at_s, rt_s], axis=0)
    kb = jnp.concatenate([kh_s, bh_s], axis=0)
    sc = _dot_nt(ar, kb)
    a_ak = jnp.where(strict, sc[0:GROUP, 0:GROUP], 0.0)
    n_ab = jnp.where(strict, sc[0:GROUP, GROUP:2 * GROUP], 0.0)
    a_rk = jnp.where(incl, sc[GROUP:2 * GROUP, 0:GROUP], 0.0)
    a_rb = jnp.where(incl, sc[GROUP:2 * GROUP, GROUP:2 * GROUP], 0.0)

    def mm(x, y):
        return _dot(x.astype(BF16), y.astype(BF16))

    dm = jnp.where(blk16, n_ab, 0.0)
    e = n_ab - dm
    p = dm
    td = eye + dm
    for _ in range(3):
        p = mm(p, p)
        td = td + mm(td, p)
    m1 = mm(td, e)
    m2 = mm(m1, m1)
    tm = mm(eye + m1 + m2 + mm(m1, m2), td)

    s_bf = s_prev.astype(BF16)
    ss = _dot_nt(ar, s_bf)
    pm = ss[0:GROUP] + _dot(a_ak.astype(BF16), v_s)
    sa = mm(tm, pm)
    sa_bf = sa.astype(BF16)
    y_s = (ss[GROUP:2 * GROUP] + _dot(a_rk.astype(BF16), v_s) + _dot(a_rb.astype(BF16), sa_bf))
    y = y_s[0:CHUNK]
    for hh in range(1, GROUP_HEADS):
        y = y + y_s[hh * CHUNK:(hh + 1) * CHUNK]
    s_next = (s_prev * jnp.exp(cl)
              + _dot_tn(v_s, kend_s) + _dot_tn(sa_bf, bend_s))
    return y, s_next


def _wkv_kernel(r_ref, lw_ref, k_ref, v_ref, kk_ref, a_ref, y_ref, s_ref, *, groups):
    c = pl.program_id(2)

    @pl.when(c == 0)
    def _():
        s_ref[...] = jnp.zeros_like(s_ref)

    ri = lax.broadcasted_iota(jnp.int32, (GROUP, GROUP), 0)
    ci = lax.broadcasted_iota(jnp.int32, (GROUP, GROUP), 1)
    same = (ri // CHUNK) == (ci // CHUNK)
    rt = ri % CHUNK
    ct = ci % CHUNK
    strict = jnp.logical_and(same, rt > ct)
    incl = jnp.logical_and(same, rt >= ct)
    blk16 = jnp.logical_and(same, (rt // SUB) == (ct // SUB))
    eye = jnp.where(ri == ci, 1.0, 0.0)
    ti = lax.broadcasted_iota(jnp.int32, (CHUNK, CHUNK), 0)
    tj = lax.broadcasted_iota(jnp.int32, (CHUNK, CHUNK), 1)
    tri = jnp.where(ti >= tj, 1.0, 0.0).astype(BF16)
    lane_head = lax.broadcasted_iota(jnp.int32, (CHUNK, GROUP), 1) // HEAD
    head_masks = [lane_head == hh for hh in range(GROUP_HEADS)]
    consts = (tri, head_masks, strict, incl, blk16, eye)

    for g in range(groups):
        sl = slice(g * GROUP, (g + 1) * GROUP)
        y, s_next = _wkv_group(s_ref[g], r_ref[:, sl], lw_ref[:, sl], k_ref[:, sl],
                               v_ref[:, sl], kk_ref[:, sl], a_ref[:, sl], consts)
        y_ref[:, sl] = y
        s_ref[g] = s_next


def _wkv(r_src, lw, k, v_src, kk, a, bsz, seq):
    n, d = lw.shape
    groups = _blk(d, 4 * GROUP, GROUP) // GROUP
    gw = groups * GROUP
    nc = seq // CHUNK

    def spec(src):
        arr, lead = src
        if lead is None:
            return arr, pl.BlockSpec((CHUNK, gw), lambda bb, j, c: (bb * nc + c, j))
        return arr, pl.BlockSpec((None, CHUNK, gw), lambda bb, j, c: (lead, bb * nc + c, j))

    ins, specs = zip(*[spec(s) for s in (r_src, (lw, None), (k, None), v_src, (kk, None), (a, None))])
    return pl.pallas_call(
        functools.partial(_wkv_kernel, groups=groups),
        grid=(bsz, d // gw, nc),
        in_specs=list(specs),
        out_specs=pl.BlockSpec((CHUNK, gw), lambda bb, j, c: (bb * nc + c, j)),
        out_shape=jax.ShapeDtypeStruct((n, d), F32),
        scratch_shapes=[pltpu.VMEM((groups, GROUP, GROUP), F32)],
        compiler_params=_cparams(("parallel", "parallel", "arbitrary")),
        name="wkv7",
    )(*ins)


def _post_kernel(y_ref, r_ref, k_ref, v_ref, g_ref, lg_ref, lb_ref, rk_ref, o_ref):
    ones = _head_ones(LANE)
    y = y_ref[...]
    inv = 1.0 / HEAD
    dlt = y - _segsum(y, ones) * inv
    var = _segsum(dlt * dlt, ones) * inv
    yn = dlt * lax.rsqrt(var + GN_EPS) * lg_ref[...] + lb_ref[...]
    bonus = _segsum(r_ref[...] * k_ref[...] * rk_ref[...], ones) * v_ref[...]
    o_ref[...] = ((yn + bonus) * g_ref[...]).astype(o_ref.dtype)


def _post(y, rkv, km, v_src, g, lnx_g, lnx_b, r_k):
    n, d = y.shape
    bm = _blk(n, 512, SUBLANE)
    bn = _blk(d, 512, LANE)
    blk = pl.BlockSpec((bm, bn), lambda i, j: (i, j))
    prow = pl.BlockSpec((1, bn), lambda i, j: (0, j))
    v_arr, v_lead = v_src
    v_spec = blk if v_lead is None else pl.BlockSpec((None, bm, bn), lambda i, j: (v_lead, i, j))
    return pl.pallas_call(
        _post_kernel,
        grid=(n // bm, d // bn),
        in_specs=[blk, pl.BlockSpec((None, bm, bn), lambda i, j: (0, i, j)), blk, v_spec, blk,
                  prow, prow, prow],
        out_specs=blk,
        out_shape=jax.ShapeDtypeStruct((n, d), BF16),
        compiler_params=_cparams(("parallel", "parallel")),
        name="rwkv_post",
    )(y, rkv, km, v_arr, g, lnx_g.reshape(1, d), lnx_b.reshape(1, d), r_k.reshape(1, d))


def _pad_to(w, axis, mult):
    size = w.shape[axis]
    target = -(-size // mult) * mult
    if target == size:
        return w
    pad = [(0, 0)] * w.ndim
    pad[axis] = (0, target - size)
    return jnp.pad(w, pad)


def kernel(x, c, ada_w, ada_b, ada_emb, ln1_g, ln2_g, mlp_w1, mlp_w2, conv_w_in, conv_w, conv_w_out, rwkv_mu, rwkv_w_rkv, rwkv_w0, rwkv_w1, rwkv_w2, rwkv_a0, rwkv_a1, rwkv_a2, rwkv_g1, rwkv_g2, rwkv_k_k, rwkv_k_a, rwkv_r_k, rwkv_lnx_g, rwkv_lnx_b, rwkv_w_o, rwkv_v0, rwkv_v1, rwkv_v2, final_g):
    bsz, seq, d = x.shape
    depth = ada_emb.shape[0]
    n = bsz * seq
    assert d % GROUP == 0 and seq % CHUNK == 0

    mod = _ada(c, ada_w, ada_b, ada_emb)
    xs = x.reshape(n, d)
    v_first = None
    for i in range(depth):
        j = i // 2
        if i % 2 == 0:
            h = _norm_mod(xs, ln1_g[i], mod, i, 1, 0, seq)
            gated = _conv_in(h, conv_w_in[j].astype(BF16), conv_w[j], seq)
            xs = _mm_resid(gated, conv_w_out[j].astype(BF16), xs, mod, i, 2, seq)
        else:
            use_v = v_first is not None
            downs = [rwkv_w1[j], rwkv_a1[j], rwkv_g1[j]]
            ups = [rwkv_w2[j], rwkv_a2[j], rwkv_g2[j]]
            if use_v:
                downs.append(rwkv_v1[j - 1])
                ups.append(rwkv_v2[j - 1])
            downs = [_pad_to(w, 1, LANE) for w in downs]
            ups = [_pad_to(w, 0, LANE) for w in ups]
            widths = tuple(w.shape[1] for w in downs) + ((0,) if not use_v else ())
            wd = jnp.concatenate(downs, axis=1).astype(BF16)
            wu = jnp.concatenate(ups, axis=0).astype(BF16)
            mixes, hid = _mix(xs, ln1_g[i], mod, i, rwkv_mu[j], wd, widths, use_v, seq)
            rkv = _rkv(mixes, rwkv_w_rkv[j].astype(BF16))
            outs = _prep(hid, wu, rkv, v_first, rwkv_w0[j], rwkv_a0[j], rwkv_k_k[j],
                         rwkv_k_a[j], rwkv_v0[j - 1] if use_v else None, widths, use_v)
            if use_v:
                lw, a, g, kkn, km, vm = outs
                v_src = (vm, None)
            else:
                lw, a, g, kkn, km = outs
                v_src = (rkv, 2)
                v_first = rkv[2]
            y = _wkv((rkv, 0), lw, km, v_src, kkn, a, bsz, seq)
            gated = _post(y, rkv, km, v_src, g, rwkv_lnx_g[j], rwkv_lnx_b[j], rwkv_r_k[j])
            xs = _mm_resid(gated, rwkv_w_o[j].astype(BF16), xs, mod, i, 2, seq)
        h = _norm_mod(xs, ln2_g[i], mod, i, 4, 3, seq)
        hidden = _mlp_up(h, mlp_w1[i].astype(BF16))
        xs = _mm_resid(hidden, mlp_w2[i].astype(BF16), xs, mod, i, 5, seq)
    return _final_norm(xs, final_g).reshape(bsz, seq, d)
```

```python
import functools

import jax
import jax.numpy as jnp
from jax import lax
from jax.experimental import pallas as pl
from jax.experimental.pallas import tpu as pltpu

F32 = jnp.float32
BF16 = jnp.bfloat16

HEAD = 64
CHUNK = 64
SUB = 16
GROUP_HEADS = 4
GROUP = GROUP_HEADS * HEAD
WKV_GROUPS = 8
LANE = 128
SUBLANE = 8
ADA_CHUNKS = 6
CONV_WIDTH = 3
RMS_EPS = 1e-5
GN_EPS = 64e-5
VMEM_LIMIT = 56 * 1024 * 1024


def _blk(n, pref, align):
    best = None
    d = align
    while d <= min(n, pref):
        if n % d == 0:
            best = d
        d += align
    return best if best is not None else n


def _cparams(sem):
    return pltpu.CompilerParams(dimension_semantics=sem, vmem_limit_bytes=VMEM_LIMIT)


def _dot(a, b):
    return jnp.dot(a, b, preferred_element_type=F32)


def _dot_nt(a, b):
    return lax.dot_general(a, b, (((1,), (1,)), ((), ())), preferred_element_type=F32)


def _dot_tn(a, b):
    return lax.dot_general(a, b, (((0,), (0,)), ((), ())), preferred_element_type=F32)


def _split3(x):
    h1 = x.astype(BF16)
    r1 = x - h1.astype(F32)
    h2 = r1.astype(BF16)
    h3 = (r1 - h2.astype(F32)).astype(BF16)
    return h1, h2, h3


def _head_ones(n):
    r = lax.broadcasted_iota(jnp.int32, (n, n), 0) // HEAD
    c = lax.broadcasted_iota(jnp.int32, (n, n), 1) // HEAD
    return jnp.where(r == c, 1.0, 0.0).astype(BF16)


def _segsum(x, ones):
    outs = []
    for s in range(x.shape[1] // LANE):
        h1, h2, h3 = _split3(x[:, s * LANE:(s + 1) * LANE])
        outs.append(_dot(h1, ones) + _dot(h2, ones) + _dot(h3, ones))
    return outs[0] if len(outs) == 1 else jnp.concatenate(outs, axis=1)


def _ada_kernel(c_ref, w_ref, b_ref, emb_ref, o_ref, *, depth):
    s = jax.nn.silu(c_ref[...])
    acc = jnp.dot(s, w_ref[...], preferred_element_type=F32,
                  precision=lax.Precision.HIGHEST) + b_ref[...]
    for i in range(depth):
        o_ref[i] = acc + emb_ref[i]


def _ada(c, ada_w, ada_b, ada_emb):
    bsz, d = c.shape
    depth = ada_emb.shape[0]
    n = ada_w.shape[1]
    rows = -(-bsz // SUBLANE) * SUBLANE
    cp = jnp.pad(c, ((0, rows - bsz), (0, 0)))
    bn = _blk(n, 512, LANE)
    out = pl.pallas_call(
        functools.partial(_ada_kernel, depth=depth),
        grid=(n // bn,),
        in_specs=[
            pl.BlockSpec((rows, d), lambda j: (0, 0)),
            pl.BlockSpec((d, bn), lambda j: (0, j)),
            pl.BlockSpec((1, bn), lambda j: (0, j)),
            pl.BlockSpec((depth, 1, bn), lambda j: (0, 0, j)),
        ],
        out_specs=pl.BlockSpec((depth, rows, bn), lambda j: (0, 0, j)),
        out_shape=jax.ShapeDtypeStruct((depth, rows, n), F32),
        compiler_params=_cparams(("parallel",)),
        name="ada_mod",
    )(cp, ada_w, ada_b.reshape(1, n), ada_emb.reshape(depth, 1, n))
    mod = out[:, :bsz].reshape(depth, bsz, ADA_CHUNKS, d)
    return jnp.transpose(mod, (0, 2, 1, 3)).reshape(depth, ADA_CHUNKS, bsz, 1, d)


def _mod_spec(layer, chunk, tpb, bn, col_axis):
    if col_axis is None:
        return pl.BlockSpec((None, None, None, 1, bn), lambda i: (layer, chunk, i // tpb, 0, 0))
    if col_axis == 1:
        return pl.BlockSpec((None, None, None, 1, bn),
                            lambda i, j, *_: (layer, chunk, i // tpb, 0, j))
    return pl.BlockSpec((None, None, None, 1, bn),
                        lambda j, i: (layer, chunk, i // tpb, 0, j))


def _rms(x, g):
    ms = jnp.mean(x * x, axis=-1, keepdims=True)
    return (x * lax.rsqrt(ms + RMS_EPS)) * g


def _norm_mod_kernel(x_ref, g_ref, sc_ref, sh_ref, o_ref):
    y = _rms(x_ref[...], g_ref[...])
    o_ref[...] = (y * (1.0 + sc_ref[...]) + sh_ref[...]).astype(o_ref.dtype)


def _norm_mod(x, g, mod, layer, sc_chunk, sh_chunk, seq):
    n, d = x.shape
    bm = _blk(seq, 256, SUBLANE)
    tpb = seq // bm
    return pl.pallas_call(
        _norm_mod_kernel,
        grid=(n // bm,),
        in_specs=[
            pl.BlockSpec((bm, d), lambda i: (i, 0)),
            pl.BlockSpec((1, d), lambda i: (0, 0)),
            _mod_spec(layer, sc_chunk, tpb, d, None),
            _mod_spec(layer, sh_chunk, tpb, d, None),
        ],
        out_specs=pl.BlockSpec((bm, d), lambda i: (i, 0)),
        out_shape=jax.ShapeDtypeStruct((n, d), BF16),
        compiler_params=_cparams(("parallel",)),
        name="norm_mod",
    )(x, g.reshape(1, d), mod, mod)


def _final_norm_kernel(x_ref, g_ref, o_ref):
    o_ref[...] = _rms(x_ref[...], g_ref[...])


def _final_norm(x, g):
    n, d = x.shape
    bm = _blk(n, 512, SUBLANE)
    return pl.pallas_call(
        _final_norm_kernel,
        grid=(n // bm,),
        in_specs=[pl.BlockSpec((bm, d), lambda i: (i, 0)),
                  pl.BlockSpec((1, d), lambda i: (0, 0))],
        out_specs=pl.BlockSpec((bm, d), lambda i: (i, 0)),
        out_shape=jax.ShapeDtypeStruct((n, d), F32),
        compiler_params=_cparams(("parallel",)),
        name="final_norm",
    )(x, g.reshape(1, d))


def _mix_kernel(x_ref, xp_ref, g_ref, sc_ref, sh_ref, mu_ref, wd_ref, mix_ref, hid_ref,
                *, tpb, widths, use_v):
    i = pl.program_id(0)

    def nm(x):
        return _rms(x, g_ref[...]) * (1.0 + sc_ref[...]) + sh_ref[...]

    h = nm(x_ref[...])
    hp = nm(xp_ref[...])[SUBLANE - 1:SUBLANE]
    hp = jnp.where(i % tpb == 0, 0.0, hp)
    rows = lax.broadcasted_iota(jnp.int32, h.shape, 0)
    xx = jnp.where(rows == 0, hp, pltpu.roll(h, 1, 0)) - h

    def mix(j):
        return (h + xx * mu_ref[j:j + 1, :]).astype(BF16)

    mix_ref[0] = mix(0)
    mix_ref[1] = mix(1)
    xv = mix(2)
    mix_ref[2] = xv
    wl, al, gl, vl = widths
    o = 0
    hid_ref[:, o:o + wl] = jnp.tanh(_dot(mix(3), wd_ref[:, o:o + wl])).astype(BF16)
    o += wl
    hid_ref[:, o:o + al] = _dot(mix(4), wd_ref[:, o:o + al]).astype(BF16)
    o += al
    hid_ref[:, o:o + gl] = jax.nn.sigmoid(_dot(mix(5), wd_ref[:, o:o + gl])).astype(BF16)
    o += gl
    if use_v:
        hid_ref[:, o:o + vl] = _dot(xv, wd_ref[:, o:o + vl]).astype(BF16)


def _mix(x, g, mod, layer, mu, wd, widths, use_v, seq):
    n, d = x.shape
    hl = wd.shape[1]
    bm = _blk(seq, 128, SUBLANE)
    tpb = seq // bm
    per8 = bm // SUBLANE
    return pl.pallas_call(
        functools.partial(_mix_kernel, tpb=tpb, widths=widths, use_v=use_v),
        grid=(n // bm,),
        in_specs=[
            pl.BlockSpec((bm, d), lambda i: (i, 0)),
            pl.BlockSpec((SUBLANE, d), lambda i: (jnp.maximum(i * per8 - 1, 0), 0)),
            pl.BlockSpec((1, d), lambda i: (0, 0)),
            _mod_spec(layer, 1, tpb, d, None),
            _mod_spec(layer, 0, tpb, d, None),
            pl.BlockSpec((6, d), lambda i: (0, 0)),
            pl.BlockSpec((d, hl), lambda i: (0, 0)),
        ],
        out_specs=[pl.BlockSpec((3, bm, d), lambda i: (0, i, 0)),
                   pl.BlockSpec((bm, hl), lambda i: (i, 0))],
        out_shape=[jax.ShapeDtypeStruct((3, n, d), BF16),
                   jax.ShapeDtypeStruct((n, hl), BF16)],
        compiler_params=_cparams(("parallel",)),
        name="rwkv_mix",
    )(x, x, g.reshape(1, d), mod, mod, mu, wd)


def _conv_in_kernel(h_ref, wb_ref, wc_ref, wu_ref, cw_ref, o_ref, carry_ref, *, tpb):
    m = pl.program_id(1)
    h = h_ref[...]
    bg = _dot(h, wb_ref[...])
    z = _dot(h, wc_ref[...]) * _dot(h, wu_ref[...])
    bm = z.shape[0]
    prev = jnp.where(m % tpb == 0, 0.0, carry_ref[...])
    carry_ref[...] = z[bm - SUBLANE:bm]
    w0 = cw_ref[0:1, :]
    w1 = cw_ref[1:2, :]
    w2 = cw_ref[2:3, :]
    y = w0 * pltpu.roll(z, 2, 0) + w1 * pltpu.roll(z, 1, 0) + w2 * z
    o_ref[...] = (bg * y).astype(o_ref.dtype)
    zt = z[0:SUBLANE]
    r8 = lax.broadcasted_iota(jnp.int32, zt.shape, 0)
    z1 = jnp.where(r8 < 1, pltpu.roll(prev, 1, 0), pltpu.roll(zt, 1, 0))
    z2 = jnp.where(r8 < 2, pltpu.roll(prev, 2, 0), pltpu.roll(zt, 2, 0))
    yt = w0 * z2 + w1 * z1 + w2 * zt
    o_ref[0:SUBLANE, :] = (bg[0:SUBLANE] * yt).astype(o_ref.dtype)


def _conv_in(h, w_in, conv_w, layer, seq):
    n, d = h.shape
    bm = _blk(seq, 1024, SUBLANE)
    bn = _blk(d, 256, LANE)
    tpb = seq // bm
    nj = d // bn
    return pl.pallas_call(
        functools.partial(_conv_in_kernel, tpb=tpb),
        grid=(nj, n // bm),
        in_specs=[
            pl.BlockSpec((bm, d), lambda j, i: (i, 0)),
            pl.BlockSpec((None, d, bn), lambda j, i: (layer, 0, j)),
            pl.BlockSpec((None, d, bn), lambda j, i: (layer, 0, j + nj)),
            pl.BlockSpec((None, d, bn), lambda j, i: (layer, 0, j + 2 * nj)),
            pl.BlockSpec((None, CONV_WIDTH, bn), lambda j, i: (layer, 0, j)),
        ],
        out_specs=pl.BlockSpec((bm, bn), lambda j, i: (i, j)),
        out_shape=jax.ShapeDtypeStruct((n, d), BF16),
        scratch_shapes=[pltpu.VMEM((SUBLANE, bn), F32)],
        compiler_params=_cparams(("arbitrary", "arbitrary")),
        name="conv_in",
    )(h, w_in, w_in, w_in, conv_w)


def _mm_resid_kernel(a_ref, w_ref, x_ref, gate_ref, o_ref, *, nk):
    if nk == 1:
        o_ref[...] = x_ref[...] + gate_ref[...] * _dot(a_ref[...], w_ref[...])
        return
    k = pl.program_id(2)

    @pl.when(k == 0)
    def _():
        o_ref[...] = jnp.zeros_like(o_ref)

    acc = o_ref[...] + _dot(a_ref[...], w_ref[...])
    o_ref[...] = jnp.where(k == nk - 1, x_ref[...] + gate_ref[...] * acc, acc)


def _mm_resid(a, w, widx, x, mod, layer, gate_chunk, seq):
    n, kdim = a.shape
    d = w.shape[2]
    bm = _blk(seq, 1024, SUBLANE)
    bk = _blk(kdim, 4096 if kdim <= 4096 else 2048, LANE)
    nk = kdim // bk
    bn = _blk(d, 512 if nk == 1 else 1024, LANE)
    tpb = seq // bm
    return pl.pallas_call(
        functools.partial(_mm_resid_kernel, nk=nk),
        grid=(n // bm, d // bn, nk),
        in_specs=[
            pl.BlockSpec((bm, bk), lambda i, j, k: (i, k)),
            pl.BlockSpec((None, bk, bn), lambda i, j, k: (widx, k, j)),
            pl.BlockSpec((bm, bn), lambda i, j, k: (i, j)),
            _mod_spec(layer, gate_chunk, tpb, bn, 1),
        ],
        out_specs=pl.BlockSpec((bm, bn), lambda i, j, k: (i, j)),
        out_shape=jax.ShapeDtypeStruct((n, d), F32),
        compiler_params=_cparams(("parallel", "parallel", "arbitrary")),
        name="mm_resid",
    )(a, w, x, mod)


def _mlp_up_kernel(a_ref, w_ref, o_ref):
    o_ref[...] = jnp.square(jnp.maximum(_dot(a_ref[...], w_ref[...]), 0.0)).astype(o_ref.dtype)


def _mlp_up(a, w, widx):
    n, kdim = a.shape
    f = w.shape[2]
    bm = _blk(n, 1024, SUBLANE)
    bn = _blk(f, 1024, LANE)
    return pl.pallas_call(
        _mlp_up_kernel,
        grid=(n // bm, f // bn),
        in_specs=[pl.BlockSpec((bm, kdim), lambda i, j: (i, 0)),
                  pl.BlockSpec((None, kdim, bn), lambda i, j: (widx, 0, j))],
        out_specs=pl.BlockSpec((bm, bn), lambda i, j: (i, j)),
        out_shape=jax.ShapeDtypeStruct((n, f), BF16),
        compiler_params=_cparams(("parallel", "parallel")),
        name="mlp_up",
    )(a, w)


def _rkv_kernel(a_ref, w_ref, o_ref):
    o_ref[...] = _dot(a_ref[...], w_ref[...])


def _rkv(mixes, w_rkv, widx):
    _, n, d = mixes.shape
    bm = _blk(n, 1024, SUBLANE)
    bn = _blk(d, 1024, LANE)
    return pl.pallas_call(
        _rkv_kernel,
        grid=(3, n // bm, d // bn),
        in_specs=[pl.BlockSpec((None, bm, d), lambda s, i, j: (s, i, 0)),
                  pl.BlockSpec((None, None, d, bn), lambda s, i, j: (widx, s, 0, j))],
        out_specs=pl.BlockSpec((None, bm, bn), lambda s, i, j: (s, i, j)),
        out_shape=jax.ShapeDtypeStruct((3, n, d), F32),
        compiler_params=_cparams(("parallel", "parallel", "parallel")),
        name="rkv_proj",
    )(mixes, w_rkv)


def _prep_kernel(*refs, widths, use_v):
    if use_v:
        (hid_ref, wu_ref, k_ref, v_ref, vf_ref, w0_ref, a0_ref, kk_ref, ka_ref, v0_ref,
         lw_ref, a_ref, g_ref, kkn_ref, km_ref, vm_ref) = refs
    else:
        (hid_ref, wu_ref, k_ref, w0_ref, a0_ref, kk_ref, ka_ref,
         lw_ref, a_ref, g_ref, kkn_ref, km_ref) = refs
    wl, al, gl, vl = widths
    o = 0
    x = w0_ref[...] + _dot(hid_ref[:, o:o + wl], wu_ref[o:o + wl, :])
    o += wl
    sp = jnp.maximum(-x, 0.0) + jnp.log(1.0 + jnp.exp(-jnp.abs(x)))
    lw_ref[...] = -jnp.exp(-sp - 0.5)
    a = jax.nn.sigmoid(a0_ref[...] + _dot(hid_ref[:, o:o + al], wu_ref[o:o + al, :]))
    o += al
    a_ref[...] = a
    g_ref[...] = _dot(hid_ref[:, o:o + gl], wu_ref[o:o + gl, :])
    o += gl
    k = k_ref[...]
    kk = k * kk_ref[...]
    ss = _segsum(kk * kk, _head_ones(LANE))
    kkn_ref[...] = kk * lax.rsqrt(jnp.maximum(ss, 1e-24))
    km_ref[...] = k * (1.0 + (a - 1.0) * ka_ref[...])
    if use_v:
        v = v_ref[...]
        vg = jax.nn.sigmoid(v0_ref[...] + _dot(hid_ref[:, o:o + vl], wu_ref[o:o + vl, :]))
        vm_ref[...] = v + (vf_ref[...] - v) * vg


def _prep(hid, wu, rkv, v_first, w0, a0, k_k, k_a, v0, widths, use_v):
    n, hl = hid.shape
    d = wu.shape[1]
    bm = _blk(n, 512, SUBLANE)
    bn = _blk(d, 512, LANE)
    row = lambda p: p.reshape(1, d)
    blk = pl.BlockSpec((bm, bn), lambda i, j: (i, j))
    prow = pl.BlockSpec((1, bn), lambda i, j: (0, j))
    ins = [hid, wu, rkv]
    specs = [pl.BlockSpec((bm, hl), lambda i, j: (i, 0)),
             pl.BlockSpec((hl, bn), lambda i, j: (0, j)),
             pl.BlockSpec((None, bm, bn), lambda i, j: (1, i, j))]
    if use_v:
        vf_arr, vf_lead = v_first
        ins += [rkv, vf_arr]
        specs += [pl.BlockSpec((None, bm, bn), lambda i, j: (2, i, j)),
                  pl.BlockSpec((None, bm, bn), lambda i, j: (vf_lead, i, j))]
    ins += [row(w0), row(a0), row(k_k), row(k_a)]
    specs += [prow] * 4
    n_out = 5
    if use_v:
        ins.append(row(v0))
        specs.append(prow)
        n_out = 6
    return pl.pallas_call(
        functools.partial(_prep_kernel, widths=widths, use_v=use_v),
        grid=(n // bm, d // bn),
        in_specs=specs,
        out_specs=[blk] * n_out,
        out_shape=[jax.ShapeDtypeStruct((n, d), F32)] * n_out,
        compiler_params=_cparams(("parallel", "parallel")),
        name="rwkv_prep",
    )(*ins)


def _wkv_chunk(s_prev, ins, consts):
    tri, head_masks, strict_w, incl_w, same, blk16, eye = consts
    ng = range(len(ins))

    def stack(x):
        return jnp.concatenate([jnp.where(m, x, 0.0) for m in head_masks], axis=0).astype(BF16)

    def mm(x, y):
        return _dot(x.astype(BF16), y.astype(BF16))

    ar, kb_s, v_s, vk_l, vk_r, gl = [], [], [], [], [], []
    for r, lw, k, v, kk, a in ins:
        h1, h2, h3 = _split3(lw)
        cum = _dot(tri, h1) + _dot(tri, h2) + _dot(tri, h3)
        cl = cum[CHUNK - 1:CHUNK]
        gi = jnp.exp(-cum)
        b = kk * a
        dl = jnp.exp(cl - cum)
        ar.append(jnp.concatenate([-(kk * jnp.exp(cum - lw)), r * jnp.exp(cum)], axis=0).astype(BF16))
        kb_s.append(jnp.concatenate([stack(k * gi), stack(b * gi)], axis=0))
        v_s.append(stack(v))
        vk_l.append(v)
        vk_r.append(jnp.concatenate([k * dl, b * dl], axis=0).astype(BF16))
        gl.append(jnp.exp(cl))

    sc = [_dot_nt(ar[g], kb_s[g]) for g in ng]
    a_ak = [jnp.where(strict_w, sc[g][0:CHUNK, 0:GROUP], 0.0).astype(BF16) for g in ng]
    a_rk = [jnp.where(incl_w, sc[g][CHUNK:2 * CHUNK, 0:GROUP], 0.0).astype(BF16) for g in ng]
    a_rb = [jnp.where(incl_w, sc[g][CHUNK:2 * CHUNK, GROUP:2 * GROUP], 0.0).astype(BF16) for g in ng]
    n_ab = [jnp.concatenate(
        [jnp.where(jnp.logical_and(strict_w, m), sc[g][0:CHUNK, GROUP:2 * GROUP], 0.0) for m in head_masks],
        axis=0) for g in ng]
    dm = [jnp.where(blk16, n_ab[g], 0.0) for g in ng]
    e = [(n_ab[g] - dm[g]).astype(BF16) for g in ng]

    ss = [_dot_nt(ar[g], s_prev[g].astype(BF16)) for g in ng]
    pm = [stack(ss[g][0:CHUNK] + _dot(a_ak[g], v_s[g])) for g in ng]
    y0 = [ss[g][CHUNK:2 * CHUNK] + _dot(a_rk[g], v_s[g]) for g in ng]

    p = dm
    td = [eye + dm[g] for g in ng]
    for _ in range(3):
        p = [mm(p[g], p[g]) for g in ng]
        td = [td[g] + mm(td[g], p[g]) for g in ng]
    td = [t.astype(BF16) for t in td]
    m1 = [_dot(td[g], e[g]) for g in ng]
    u = [_dot(td[g], pm[g]) for g in ng]
    m2 = [mm(m1[g], m1[g]) for g in ng]
    m3 = [mm(m1[g], m2[g]) for g in ng]
    sa = [mm(eye + m1[g] + m2[g] + m3[g], u[g]) for g in ng]

    y, s_next = [], []
    for g in ng:
        y.append(y0[g] + _dot(a_rb[g], sa[g].astype(BF16)))
    for g in ng:
        sa_n = sa[g][0:CHUNK]
        for hh in range(1, GROUP_HEADS):
            sa_n = sa_n + sa[g][hh * CHUNK:(hh + 1) * CHUNK]
        lhs = jnp.concatenate([vk_l[g], sa_n], axis=0).astype(BF16)
        upd = _dot_tn(lhs, vk_r[g])
        s_next.append(s_prev[g] * gl[g] + jnp.where(same, upd, 0.0))
    return y, s_next


def _wkv_kernel(r_ref, lw_ref, k_ref, v_ref, kk_ref, a_ref, y_ref, s_ref, *, groups):
    c = pl.program_id(2)

    @pl.when(c == 0)
    def _():
        s_ref[...] = jnp.zeros_like(s_ref)

    ri = lax.broadcasted_iota(jnp.int32, (GROUP, GROUP), 0)
    ci = lax.broadcasted_iota(jnp.int32, (GROUP, GROUP), 1)
    same = (ri // CHUNK) == (ci // CHUNK)
    blk16 = (ri // SUB) == (ci // SUB)
    eye = jnp.where(ri == ci, 1.0, 0.0)
    wt = lax.broadcasted_iota(jnp.int32, (CHUNK, GROUP), 0)
    ws = lax.broadcasted_iota(jnp.int32, (CHUNK, GROUP), 1) % CHUNK
    strict_w = wt > ws
    incl_w = wt >= ws
    ti = lax.broadcasted_iota(jnp.int32, (CHUNK, CHUNK), 0)
    tj = lax.broadcasted_iota(jnp.int32, (CHUNK, CHUNK), 1)
    tri = jnp.where(ti >= tj, 1.0, 0.0).astype(BF16)
    lane_head = lax.broadcasted_iota(jnp.int32, (CHUNK, GROUP), 1) // HEAD
    head_masks = [lane_head == hh for hh in range(GROUP_HEADS)]
    consts = (tri, head_masks, strict_w, incl_w, same, blk16, eye)

    cols = [slice(g * GROUP, (g + 1) * GROUP) for g in range(groups)]
    ins = [tuple(ref[:, sl] for ref in (r_ref, lw_ref, k_ref, v_ref, kk_ref, a_ref)) for sl in cols]
    y, s_next = _wkv_chunk([s_ref[g] for g in range(groups)], ins, consts)
    for g, sl in enumerate(cols):
        y_ref[:, sl] = y[g]
        s_ref[g] = s_next[g]


def _wkv(r_src, lw, k, v_src, kk, a, bsz, seq):
    n, d = lw.shape
    groups = _blk(d, WKV_GROUPS * GROUP, GROUP) // GROUP
    gw = groups * GROUP
    nc = seq // CHUNK

    def spec(src):
        arr, lead = src
        if lead is None:
            return arr, pl.BlockSpec((CHUNK, gw), lambda bb, j, c: (bb * nc + c, j))
        return arr, pl.BlockSpec((None, CHUNK, gw), lambda bb, j, c: (lead, bb * nc + c, j))

    ins, specs = zip(*[spec(s) for s in (r_src, (lw, None), (k, None), v_src, (kk, None), (a, None))])
    return pl.pallas_call(
        functools.partial(_wkv_kernel, groups=groups),
        grid=(bsz, d // gw, nc),
        in_specs=list(specs),
        out_specs=pl.BlockSpec((CHUNK, gw), lambda bb, j, c: (bb * nc + c, j)),
        out_shape=jax.ShapeDtypeStruct((n, d), F32),
        scratch_shapes=[pltpu.VMEM((groups, GROUP, GROUP), F32)],
        compiler_params=_cparams(("parallel", "parallel", "arbitrary")),
        name="wkv7",
    )(*ins)


def _post_kernel(y_ref, r_ref, k_ref, v_ref, g_ref, lg_ref, lb_ref, rk_ref, o_ref):
    ones = _head_ones(LANE)
    y = y_ref[...]
    inv = 1.0 / HEAD
    dlt = y - _segsum(y, ones) * inv
    var = _segsum(dlt * dlt, ones) * inv
    yn = dlt * lax.rsqrt(var + GN_EPS) * lg_ref[...] + lb_ref[...]
    bonus = _segsum(r_ref[...] * k_ref[...] * rk_ref[...], ones) * v_ref[...]
    o_ref[...] = ((yn + bonus) * g_ref[...]).astype(o_ref.dtype)


def _post(y, rkv, km, v_src, g, lnx_g, lnx_b, r_k):
    n, d = y.shape
    bm = _blk(n, 512, SUBLANE)
    bn = _blk(d, 512, LANE)
    blk = pl.BlockSpec((bm, bn), lambda i, j: (i, j))
    prow = pl.BlockSpec((1, bn), lambda i, j: (0, j))
    v_arr, v_lead = v_src
    v_spec = blk if v_lead is None else pl.BlockSpec((None, bm, bn), lambda i, j: (v_lead, i, j))
    return pl.pallas_call(
        _post_kernel,
        grid=(n // bm, d // bn),
        in_specs=[blk, pl.BlockSpec((None, bm, bn), lambda i, j: (0, i, j)), blk, v_spec, blk,
                  prow, prow, prow],
        out_specs=blk,
        out_shape=jax.ShapeDtypeStruct((n, d), BF16),
        compiler_params=_cparams(("parallel", "parallel")),
        name="rwkv_post",
    )(y, rkv, km, v_arr, g, lnx_g.reshape(1, d), lnx_b.reshape(1, d), r_k.reshape(1, d))


def _pad_to(w, axis, mult):
    size = w.shape[axis]
    target = -(-size // mult) * mult
    if target == size:
        return w
    pad = [(0, 0)] * w.ndim
    pad[axis] = (0, target - size)
    return jnp.pad(w, pad)


def kernel(x, c, ada_w, ada_b, ada_emb, ln1_g, ln2_g, mlp_w1, mlp_w2, conv_w_in, conv_w, conv_w_out, rwkv_mu, rwkv_w_rkv, rwkv_w0, rwkv_w1, rwkv_w2, rwkv_a0, rwkv_a1, rwkv_a2, rwkv_g1, rwkv_g2, rwkv_k_k, rwkv_k_a, rwkv_r_k, rwkv_lnx_g, rwkv_lnx_b, rwkv_w_o, rwkv_v0, rwkv_v1, rwkv_v2, final_g):
    bsz, seq, d = x.shape
    depth = ada_emb.shape[0]
    n = bsz * seq
    assert d % GROUP == 0 and seq % CHUNK == 0

    mod = _ada(c, ada_w, ada_b, ada_emb)
    w1_b, w2_b = mlp_w1.astype(BF16), mlp_w2.astype(BF16)
    cin_b, cout_b = conv_w_in.astype(BF16), conv_w_out.astype(BF16)
    rkv_b, wo_b = rwkv_w_rkv.astype(BF16), rwkv_w_o.astype(BF16)
    xs = x.reshape(n, d)
    v_first = None
    for i in range(depth):
        j = i // 2
        if i % 2 == 0:
            h = _norm_mod(xs, ln1_g[i], mod, i, 1, 0, seq)
            gated = _conv_in(h, cin_b, conv_w, j, seq)
            xs = _mm_resid(gated, cout_b, j, xs, mod, i, 2, seq)
        else:
            use_v = v_first is not None
            downs = [rwkv_w1[j], rwkv_a1[j], rwkv_g1[j]]
            ups = [rwkv_w2[j], rwkv_a2[j], rwkv_g2[j]]
            if use_v:
                downs.append(rwkv_v1[j - 1])
                ups.append(rwkv_v2[j - 1])
            downs = [_pad_to(w, 1, LANE) for w in downs]
            ups = [_pad_to(w, 0, LANE) for w in ups]
            widths = tuple(w.shape[1] for w in downs) + ((0,) if not use_v else ())
            wd = jnp.concatenate(downs, axis=1).astype(BF16)
            wu = jnp.concatenate(ups, axis=0).astype(BF16)
            mixes, hid = _mix(xs, ln1_g[i], mod, i, rwkv_mu[j], wd, widths, use_v, seq)
            rkv = _rkv(mixes, rkv_b, j)
            outs = _prep(hid, wu, rkv, v_first, rwkv_w0[j], rwkv_a0[j], rwkv_k_k[j],
                         rwkv_k_a[j], rwkv_v0[j - 1] if use_v else None, widths, use_v)
            if use_v:
                lw, a, g, kkn, km, vm = outs
                v_src = (vm, None)
            else:
                lw, a, g, kkn, km = outs
                v_src = (rkv, 2)
                v_first = (rkv, 2)
            y = _wkv((rkv, 0), lw, km, v_src, kkn, a, bsz, seq)
            gated = _post(y, rkv, km, v_src, g, rwkv_lnx_g[j], rwkv_lnx_b[j], rwkv_r_k[j])
            xs = _mm_resid(gated, wo_b, j, xs, mod, i, 2, seq)
        h = _norm_mod(xs, ln2_g[i], mod, i, 4, 3, seq)
        hidden = _mlp_up(h, w1_b, i)
        xs = _mm_resid(hidden, w2_b, i, xs, mod, i, 5, seq)
    return _final_norm(xs, final_g).reshape(bsz, seq, d)
```

```python
import functools

import jax
import jax.numpy as jnp
from jax import lax
from jax.experimental import pallas as pl
from jax.experimental.pallas import tpu as pltpu

F32 = jnp.float32
BF16 = jnp.bfloat16

HEAD = 64
CHUNK = 64
SUB = 16
GROUP_HEADS = 4
GROUP = GROUP_HEADS * HEAD
WKV_GROUPS = 8
LANE = 128
SUBLANE = 8
ADA_CHUNKS = 6
CONV_WIDTH = 3
RMS_EPS = 1e-5
GN_EPS = 64e-5
VMEM_LIMIT = 56 * 1024 * 1024


def _blk(n, pref, align):
    best = None
    d = align
    while d <= min(n, pref):
        if n % d == 0:
            best = d
        d += align
    return best if best is not None else n


def _cparams(sem):
    return pltpu.CompilerParams(dimension_semantics=sem, vmem_limit_bytes=VMEM_LIMIT)


def _dot(a, b):
    return jnp.dot(a, b, preferred_element_type=F32)


def _dot_nt(a, b):
    return lax.dot_general(a, b, (((1,), (1,)), ((), ())), preferred_element_type=F32)


def _dot_tn(a, b):
    return lax.dot_general(a, b, (((0,), (0,)), ((), ())), preferred_element_type=F32)


def _split3(x):
    h1 = x.astype(BF16)
    r1 = x - h1.astype(F32)
    h2 = r1.astype(BF16)
    h3 = (r1 - h2.astype(F32)).astype(BF16)
    return h1, h2, h3


def _head_ones(n):
    r = lax.broadcasted_iota(jnp.int32, (n, n), 0) // HEAD
    c = lax.broadcasted_iota(jnp.int32, (n, n), 1) // HEAD
    return jnp.where(r == c, 1.0, 0.0).astype(BF16)


def _segsum(x, ones):
    outs = []
    for s in range(x.shape[1] // LANE):
        h1, h2, h3 = _split3(x[:, s * LANE:(s + 1) * LANE])
        outs.append(_dot(h1, ones) + _dot(h2, ones) + _dot(h3, ones))
    return outs[0] if len(outs) == 1 else jnp.concatenate(outs, axis=1)


def _ada_kernel(c_ref, w_ref, b_ref, emb_ref, o_ref, *, depth):
    s = jax.nn.silu(c_ref[...])
    acc = jnp.dot(s, w_ref[...], preferred_element_type=F32,
                  precision=lax.Precision.HIGHEST) + b_ref[...]
    for i in range(depth):
        o_ref[i] = acc + emb_ref[i]


def _ada(c, ada_w, ada_b, ada_emb):
    bsz, d = c.shape
    depth = ada_emb.shape[0]
    n = ada_w.shape[1]
    rows = -(-bsz // SUBLANE) * SUBLANE
    cp = jnp.pad(c, ((0, rows - bsz), (0, 0)))
    bn = _blk(n, 512, LANE)
    out = pl.pallas_call(
        functools.partial(_ada_kernel, depth=depth),
        grid=(n // bn,),
        in_specs=[
            pl.BlockSpec((rows, d), lambda j: (0, 0)),
            pl.BlockSpec((d, bn), lambda j: (0, j)),
            pl.BlockSpec((1, bn), lambda j: (0, j)),
            pl.BlockSpec((depth, 1, bn), lambda j: (0, 0, j)),
        ],
        out_specs=pl.BlockSpec((depth, rows, bn), lambda j: (0, 0, j)),
        out_shape=jax.ShapeDtypeStruct((depth, rows, n), F32),
        compiler_params=_cparams(("parallel",)),
        name="ada_mod",
    )(cp, ada_w, ada_b.reshape(1, n), ada_emb.reshape(depth, 1, n))
    mod = out[:, :bsz].reshape(depth, bsz, ADA_CHUNKS, d)
    return jnp.transpose(mod, (0, 2, 1, 3)).reshape(depth, ADA_CHUNKS, bsz, 1, d)


def _mod_spec(layer, chunk, tpb, bn, col_axis):
    if col_axis is None:
        return pl.BlockSpec((None, None, None, 1, bn), lambda i: (layer, chunk, i // tpb, 0, 0))
    if col_axis == 1:
        return pl.BlockSpec((None, None, None, 1, bn),
                            lambda i, j, *_: (layer, chunk, i // tpb, 0, j))
    return pl.BlockSpec((None, None, None, 1, bn),
                        lambda j, i: (layer, chunk, i // tpb, 0, j))


def _rms(x, g):
    ms = jnp.mean(x * x, axis=-1, keepdims=True)
    return (x * lax.rsqrt(ms + RMS_EPS)) * g


def _norm_mod_kernel(x_ref, g_ref, sc_ref, sh_ref, o_ref):
    y = _rms(x_ref[...], g_ref[...])
    o_ref[...] = (y * (1.0 + sc_ref[...]) + sh_ref[...]).astype(o_ref.dtype)


def _norm_mod(x, g, mod, layer, sc_chunk, sh_chunk, seq):
    n, d = x.shape
    bm = _blk(seq, 256, SUBLANE)
    tpb = seq // bm
    return pl.pallas_call(
        _norm_mod_kernel,
        grid=(n // bm,),
        in_specs=[
            pl.BlockSpec((bm, d), lambda i: (i, 0)),
            pl.BlockSpec((1, d), lambda i: (0, 0)),
            _mod_spec(layer, sc_chunk, tpb, d, None),
            _mod_spec(layer, sh_chunk, tpb, d, None),
        ],
        out_specs=pl.BlockSpec((bm, d), lambda i: (i, 0)),
        out_shape=jax.ShapeDtypeStruct((n, d), BF16),
        compiler_params=_cparams(("parallel",)),
        name="norm_mod",
    )(x, g.reshape(1, d), mod, mod)


def _final_norm_kernel(x_ref, g_ref, o_ref):
    o_ref[...] = _rms(x_ref[...], g_ref[...])


def _final_norm(x, g):
    n, d = x.shape
    bm = _blk(n, 512, SUBLANE)
    return pl.pallas_call(
        _final_norm_kernel,
        grid=(n // bm,),
        in_specs=[pl.BlockSpec((bm, d), lambda i: (i, 0)),
                  pl.BlockSpec((1, d), lambda i: (0, 0))],
        out_specs=pl.BlockSpec((bm, d), lambda i: (i, 0)),
        out_shape=jax.ShapeDtypeStruct((n, d), F32),
        compiler_params=_cparams(("parallel",)),
        name="final_norm",
    )(x, g.reshape(1, d))


def _mix_kernel(x_ref, xp_ref, g_ref, sc_ref, sh_ref, mu_ref, wd_ref, mix_ref, hid_ref,
                *, tpb, widths, use_v):
    i = pl.program_id(0)

    def nm(x):
        return _rms(x, g_ref[...]) * (1.0 + sc_ref[...]) + sh_ref[...]

    h = nm(x_ref[...])
    hp = nm(xp_ref[...])[SUBLANE - 1:SUBLANE]
    hp = jnp.where(i % tpb == 0, 0.0, hp)
    rows = lax.broadcasted_iota(jnp.int32, h.shape, 0)
    xx = jnp.where(rows == 0, hp, pltpu.roll(h, 1, 0)) - h

    def mix(j):
        return (h + xx * mu_ref[j:j + 1, :]).astype(BF16)

    mix_ref[0] = mix(0)
    mix_ref[1] = mix(1)
    xv = mix(2)
    mix_ref[2] = xv
    wl, al, gl, vl = widths
    o = 0
    hid_ref[:, o:o + wl] = jnp.tanh(_dot(mix(3), wd_ref[:, o:o + wl])).astype(BF16)
    o += wl
    hid_ref[:, o:o + al] = _dot(mix(4), wd_ref[:, o:o + al]).astype(BF16)
    o += al
    hid_ref[:, o:o + gl] = jax.nn.sigmoid(_dot(mix(5), wd_ref[:, o:o + gl])).astype(BF16)
    o += gl
    if use_v:
        hid_ref[:, o:o + vl] = _dot(xv, wd_ref[:, o:o + vl]).astype(BF16)


def _mix(x, g, mod, layer, mu, wd, widths, use_v, seq):
    n, d = x.shape
    hl = wd.shape[1]
    bm = _blk(seq, 128, SUBLANE)
    tpb = seq // bm
    per8 = bm // SUBLANE
    return pl.pallas_call(
        functools.partial(_mix_kernel, tpb=tpb, widths=widths, use_v=use_v),
        grid=(n // bm,),
        in_specs=[
            pl.BlockSpec((bm, d), lambda i: (i, 0)),
            pl.BlockSpec((SUBLANE, d), lambda i: (jnp.maximum(i * per8 - 1, 0), 0)),
            pl.BlockSpec((1, d), lambda i: (0, 0)),
            _mod_spec(layer, 1, tpb, d, None),
            _mod_spec(layer, 0, tpb, d, None),
            pl.BlockSpec((6, d), lambda i: (0, 0)),
            pl.BlockSpec((d, hl), lambda i: (0, 0)),
        ],
        out_specs=[pl.BlockSpec((3, bm, d), lambda i: (0, i, 0)),
                   pl.BlockSpec((bm, hl), lambda i: (i, 0))],
        out_shape=[jax.ShapeDtypeStruct((3, n, d), BF16),
                   jax.ShapeDtypeStruct((n, hl), BF16)],
        compiler_params=_cparams(("parallel",)),
        name="rwkv_mix",
    )(x, x, g.reshape(1, d), mod, mod, mu, wd)


def _conv_in_kernel(h_ref, wb_ref, wc_ref, wu_ref, cw_ref, o_ref, carry_ref, *, tpb):
    m = pl.program_id(1)
    h = h_ref[...]
    bg = _dot(h, wb_ref[...])
    z = _dot(h, wc_ref[...]) * _dot(h, wu_ref[...])
    bm = z.shape[0]
    prev = jnp.where(m % tpb == 0, 0.0, carry_ref[...])
    carry_ref[...] = z[bm - SUBLANE:bm]
    w0 = cw_ref[0:1, :]
    w1 = cw_ref[1:2, :]
    w2 = cw_ref[2:3, :]
    y = w0 * pltpu.roll(z, 2, 0) + w1 * pltpu.roll(z, 1, 0) + w2 * z
    o_ref[...] = (bg * y).astype(o_ref.dtype)
    zt = z[0:SUBLANE]
    r8 = lax.broadcasted_iota(jnp.int32, zt.shape, 0)
    z1 = jnp.where(r8 < 1, pltpu.roll(prev, 1, 0), pltpu.roll(zt, 1, 0))
    z2 = jnp.where(r8 < 2, pltpu.roll(prev, 2, 0), pltpu.roll(zt, 2, 0))
    yt = w0 * z2 + w1 * z1 + w2 * zt
    o_ref[0:SUBLANE, :] = (bg[0:SUBLANE] * yt).astype(o_ref.dtype)


def _conv_in(h, w_in, conv_w, layer, seq):
    n, d = h.shape
    bm = _blk(seq, 1024, SUBLANE)
    bn = _blk(d, 256, LANE)
    tpb = seq // bm
    nj = d // bn
    return pl.pallas_call(
        functools.partial(_conv_in_kernel, tpb=tpb),
        grid=(nj, n // bm),
        in_specs=[
            pl.BlockSpec((bm, d), lambda j, i: (i, 0)),
            pl.BlockSpec((None, d, bn), lambda j, i: (layer, 0, j)),
            pl.BlockSpec((None, d, bn), lambda j, i: (layer, 0, j + nj)),
            pl.BlockSpec((None, d, bn), lambda j, i: (layer, 0, j + 2 * nj)),
            pl.BlockSpec((None, CONV_WIDTH, bn), lambda j, i: (layer, 0, j)),
        ],
        out_specs=pl.BlockSpec((bm, bn), lambda j, i: (i, j)),
        out_shape=jax.ShapeDtypeStruct((n, d), BF16),
        scratch_shapes=[pltpu.VMEM((SUBLANE, bn), F32)],
        compiler_params=_cparams(("arbitrary", "arbitrary")),
        name="conv_in",
    )(h, w_in, w_in, w_in, conv_w)


def _mm_resid_kernel(a_ref, w_ref, x_ref, gate_ref, o_ref, *, nk):
    if nk == 1:
        o_ref[...] = x_ref[...] + gate_ref[...] * _dot(a_ref[...], w_ref[...])
        return
    k = pl.program_id(2)

    @pl.when(k == 0)
    def _():
        o_ref[...] = jnp.zeros_like(o_ref)

    acc = o_ref[...] + _dot(a_ref[...], w_ref[...])
    o_ref[...] = jnp.where(k == nk - 1, x_ref[...] + gate_ref[...] * acc, acc)


def _mm_resid(a, w, widx, x, mod, layer, gate_chunk, seq):
    n, kdim = a.shape
    d = w.shape[2]
    deep = kdim > 4096
    bm = _blk(seq, 512 if deep else 1024, SUBLANE)
    bn = _blk(d, 256 if deep else 512, LANE)
    bk = _blk(kdim, 16384, LANE)
    nk = kdim // bk
    tpb = seq // bm
    return pl.pallas_call(
        functools.partial(_mm_resid_kernel, nk=nk),
        grid=(n // bm, d // bn, nk),
        in_specs=[
            pl.BlockSpec((bm, bk), lambda i, j, k: (i, k)),
            pl.BlockSpec((None, bk, bn), lambda i, j, k: (widx, k, j)),
            pl.BlockSpec((bm, bn), lambda i, j, k: (i, j)),
            _mod_spec(layer, gate_chunk, tpb, bn, 1),
        ],
        out_specs=pl.BlockSpec((bm, bn), lambda i, j, k: (i, j)),
        out_shape=jax.ShapeDtypeStruct((n, d), F32),
        compiler_params=_cparams(("parallel", "parallel", "arbitrary")),
        name="mm_resid",
    )(a, w, x, mod)


def _mlp_up_kernel(a_ref, w_ref, o_ref):
    o_ref[...] = jnp.square(jnp.maximum(_dot(a_ref[...], w_ref[...]), 0.0)).astype(o_ref.dtype)


def _mlp_up(a, w, widx):
    n, kdim = a.shape
    f = w.shape[2]
    bm = _blk(n, 1024, SUBLANE)
    bn = _blk(f, 1024, LANE)
    return pl.pallas_call(
        _mlp_up_kernel,
        grid=(n // bm, f // bn),
        in_specs=[pl.BlockSpec((bm, kdim), lambda i, j: (i, 0)),
                  pl.BlockSpec((None, kdim, bn), lambda i, j: (widx, 0, j))],
        out_specs=pl.BlockSpec((bm, bn), lambda i, j: (i, j)),
        out_shape=jax.ShapeDtypeStruct((n, f), BF16),
        compiler_params=_cparams(("parallel", "parallel")),
        name="mlp_up",
    )(a, w)


def _rkv_kernel(a_ref, w_ref, o_ref):
    o_ref[...] = _dot(a_ref[...], w_ref[...])


def _rkv(mixes, w_rkv, widx):
    _, n, d = mixes.shape
    bm = _blk(n, 1024, SUBLANE)
    bn = _blk(d, 1024, LANE)
    return pl.pallas_call(
        _rkv_kernel,
        grid=(3, n // bm, d // bn),
        in_specs=[pl.BlockSpec((None, bm, d), lambda s, i, j: (s, i, 0)),
                  pl.BlockSpec((None, None, d, bn), lambda s, i, j: (widx, s, 0, j))],
        out_specs=pl.BlockSpec((None, bm, bn), lambda s, i, j: (s, i, j)),
        out_shape=jax.ShapeDtypeStruct((3, n, d), F32),
        compiler_params=_cparams(("parallel", "parallel", "parallel")),
        name="rkv_proj",
    )(mixes, w_rkv)


def _prep_kernel(*refs, widths, use_v):
    if use_v:
        (hid_ref, wu_ref, k_ref, v_ref, vf_ref, w0_ref, a0_ref, kk_ref, ka_ref, v0_ref,
         lw_ref, a_ref, g_ref, kkn_ref, km_ref, vm_ref) = refs
    else:
        (hid_ref, wu_ref, k_ref, w0_ref, a0_ref, kk_ref, ka_ref,
         lw_ref, a_ref, g_ref, kkn_ref, km_ref) = refs
    wl, al, gl, vl = widths
    o = 0
    x = w0_ref[...] + _dot(hid_ref[:, o:o + wl], wu_ref[o:o + wl, :])
    o += wl
    sp = jnp.maximum(-x, 0.0) + jnp.log(1.0 + jnp.exp(-jnp.abs(x)))
    lw_ref[...] = -jnp.exp(-sp - 0.5)
    a = jax.nn.sigmoid(a0_ref[...] + _dot(hid_ref[:, o:o + al], wu_ref[o:o + al, :]))
    o += al
    a_ref[...] = a
    g_ref[...] = _dot(hid_ref[:, o:o + gl], wu_ref[o:o + gl, :])
    o += gl
    k = k_ref[...]
    kk = k * kk_ref[...]
    ss = _segsum(kk * kk, _head_ones(LANE))
    kkn_ref[...] = kk * lax.rsqrt(jnp.maximum(ss, 1e-24))
    km_ref[...] = k * (1.0 + (a - 1.0) * ka_ref[...])
    if use_v:
        v = v_ref[...]
        vg = jax.nn.sigmoid(v0_ref[...] + _dot(hid_ref[:, o:o + vl], wu_ref[o:o + vl, :]))
        vm_ref[...] = v + (vf_ref[...] - v) * vg


def _prep(hid, wu, rkv, v_first, w0, a0, k_k, k_a, v0, widths, use_v):
    n, hl = hid.shape
    d = wu.shape[1]
    bm = _blk(n, 512, SUBLANE)
    bn = _blk(d, 512, LANE)
    row = lambda p: p.reshape(1, d)
    blk = pl.BlockSpec((bm, bn), lambda i, j: (i, j))
    prow = pl.BlockSpec((1, bn), lambda i, j: (0, j))
    ins = [hid, wu, rkv]
    specs = [pl.BlockSpec((bm, hl), lambda i, j: (i, 0)),
             pl.BlockSpec((hl, bn), lambda i, j: (0, j)),
             pl.BlockSpec((None, bm, bn), lambda i, j: (1, i, j))]
    if use_v:
        vf_arr, vf_lead = v_first
        ins += [rkv, vf_arr]
        specs += [pl.BlockSpec((None, bm, bn), lambda i, j: (2, i, j)),
                  pl.BlockSpec((None, bm, bn), lambda i, j: (vf_lead, i, j))]
    ins += [row(w0), row(a0), row(k_k), row(k_a)]
    specs += [prow] * 4
    n_out = 5
    if use_v:
        ins.append(row(v0))
        specs.append(prow)
        n_out = 6
    return pl.pallas_call(
        functools.partial(_prep_kernel, widths=widths, use_v=use_v),
        grid=(n // bm, d // bn),
        in_specs=specs,
        out_specs=[blk] * n_out,
        out_shape=[jax.ShapeDtypeStruct((n, d), F32)] * n_out,
        compiler_params=_cparams(("parallel", "parallel")),
        name="rwkv_prep",
    )(*ins)


def _wkv_chunk(s_prev, ins, consts):
    tri, head_masks, strict_w, incl_w, blk_w, same, eye = consts
    ng = range(len(ins))

    def stack(x):
        return jnp.concatenate([jnp.where(m, x, 0.0) for m in head_masks], axis=0).astype(BF16)

    def bdot(x, y):
        return _dot(x, y).astype(BF16)

    ar, kb_s, v_s, vk_l, vk_r, gl = [], [], [], [], [], []
    for r, lw, k, v, kk, a in ins:
        h1, h2, h3 = _split3(lw)
        cum = _dot(tri, h1) + _dot(tri, h2) + _dot(tri, h3)
        cl = cum[CHUNK - 1:CHUNK]
        gi = jnp.exp(-cum)
        b = kk * a
        dl = jnp.exp(cl - cum)
        ar.append(jnp.concatenate([-(kk * jnp.exp(cum - lw)), r * jnp.exp(cum)], axis=0).astype(BF16))
        kb_s.append(jnp.concatenate([stack(k * gi), stack(b * gi)], axis=0))
        v_s.append(stack(v))
        vk_l.append(v.astype(BF16))
        vk_r.append(jnp.concatenate([k * dl, b * dl], axis=0).astype(BF16))
        gl.append(jnp.exp(cl))

    sc = [_dot_nt(ar[g], kb_s[g]) for g in ng]
    a_ak = [jnp.where(strict_w, sc[g][0:CHUNK, 0:GROUP], 0.0).astype(BF16) for g in ng]
    a_rk = [jnp.where(incl_w, sc[g][CHUNK:2 * CHUNK, 0:GROUP], 0.0).astype(BF16) for g in ng]
    a_rb = [jnp.where(incl_w, sc[g][CHUNK:2 * CHUNK, GROUP:2 * GROUP], 0.0).astype(BF16) for g in ng]
    n_w = [jnp.where(strict_w, sc[g][0:CHUNK, GROUP:2 * GROUP], 0.0) for g in ng]
    dm = [stack(jnp.where(blk_w, n_w[g], 0.0)) for g in ng]
    e = [stack(jnp.where(blk_w, 0.0, n_w[g])) for g in ng]

    ss = [_dot_nt(ar[g], s_prev[g].astype(BF16)) for g in ng]
    pm = [stack(ss[g][0:CHUNK] + _dot(a_ak[g], v_s[g])) for g in ng]
    y0 = [ss[g][CHUNK:2 * CHUNK] + _dot(a_rk[g], v_s[g]) for g in ng]

    p = dm
    td = [eye + dm[g] for g in ng]
    for _ in range(3):
        p = [bdot(p[g], p[g]) for g in ng]
        td = [(td[g].astype(F32) + _dot(td[g], p[g])).astype(BF16) for g in ng]
    m1 = [bdot(td[g], e[g]) for g in ng]
    u = [bdot(td[g], pm[g]) for g in ng]
    m2 = [bdot(m1[g], m1[g]) for g in ng]
    w = [(u[g].astype(F32) + _dot(m2[g], u[g])).astype(BF16) for g in ng]
    sa = [(w[g].astype(F32) + _dot(m1[g], w[g])).astype(BF16) for g in ng]

    y, s_next = [], []
    for g in ng:
        y.append(y0[g] + _dot(a_rb[g], sa[g]))
    for g in ng:
        sa_n = sa[g][0:CHUNK]
        for hh in range(1, GROUP_HEADS):
            sa_n = sa_n + sa[g][hh * CHUNK:(hh + 1) * CHUNK]
        lhs = jnp.concatenate([vk_l[g], sa_n], axis=0)
        upd = _dot_tn(lhs, vk_r[g])
        s_next.append(s_prev[g] * gl[g] + jnp.where(same, upd, 0.0))
    return y, s_next


def _wkv_kernel(r_ref, lw_ref, k_ref, v_ref, kk_ref, a_ref, y_ref, s_ref, *, groups):
    c = pl.program_id(2)

    @pl.when(c == 0)
    def _():
        s_ref[...] = jnp.zeros_like(s_ref)

    ri = lax.broadcasted_iota(jnp.int32, (GROUP, GROUP), 0)
    ci = lax.broadcasted_iota(jnp.int32, (GROUP, GROUP), 1)
    same = (ri // CHUNK) == (ci // CHUNK)
    eye = jnp.where(ri == ci, 1.0, 0.0).astype(BF16)
    wt = lax.broadcasted_iota(jnp.int32, (CHUNK, GROUP), 0)
    ws = lax.broadcasted_iota(jnp.int32, (CHUNK, GROUP), 1) % CHUNK
    strict_w = wt > ws
    incl_w = wt >= ws
    blk_w = (wt // SUB) == (ws // SUB)
    ti = lax.broadcasted_iota(jnp.int32, (CHUNK, CHUNK), 0)
    tj = lax.broadcasted_iota(jnp.int32, (CHUNK, CHUNK), 1)
    tri = jnp.where(ti >= tj, 1.0, 0.0).astype(BF16)
    lane_head = lax.broadcasted_iota(jnp.int32, (CHUNK, GROUP), 1) // HEAD
    head_masks = [lane_head == hh for hh in range(GROUP_HEADS)]
    consts = (tri, head_masks, strict_w, incl_w, blk_w, same, eye)

    cols = [slice(g * GROUP, (g + 1) * GROUP) for g in range(groups)]
    ins = [tuple(ref[:, sl] for ref in (r_ref, lw_ref, k_ref, v_ref, kk_ref, a_ref)) for sl in cols]
    y, s_next = _wkv_chunk([s_ref[g] for g in range(groups)], ins, consts)
    for g, sl in enumerate(cols):
        y_ref[:, sl] = y[g]
        s_ref[g] = s_next[g]


def _wkv(r_src, lw, k, v_src, kk, a, bsz, seq):
    n, d = lw.shape
    groups = _blk(d, WKV_GROUPS * GROUP, GROUP) // GROUP
    gw = groups * GROUP
    nc = seq // CHUNK

    def spec(src):
        arr, lead = src
        if lead is None:
            return arr, pl.BlockSpec((CHUNK, gw), lambda bb, j, c: (bb * nc + c, j))
        return arr, pl.BlockSpec((None, CHUNK, gw), lambda bb, j, c: (lead, bb * nc + c, j))

    ins, specs = zip(*[spec(s) for s in (r_src, (lw, None), (k, None), v_src, (kk, None), (a, None))])
    return pl.pallas_call(
        functools.partial(_wkv_kernel, groups=groups),
        grid=(bsz, d // gw, nc),
        in_specs=list(specs),
        out_specs=pl.BlockSpec((CHUNK, gw), lambda bb, j, c: (bb * nc + c, j)),
        out_shape=jax.ShapeDtypeStruct((n, d), F32),
        scratch_shapes=[pltpu.VMEM((groups, GROUP, GROUP), F32)],
        compiler_params=_cparams(("parallel", "parallel", "arbitrary")),
        name="wkv7",
    )(*ins)


def _post_kernel(y_ref, r_ref, k_ref, v_ref, g_ref, lg_ref, lb_ref, rk_ref, o_ref):
    ones = _head_ones(LANE)
    y = y_ref[...]
    inv = 1.0 / HEAD
    dlt = y - _segsum(y, ones) * inv
    var = _segsum(dlt * dlt, ones) * inv
    yn = dlt * lax.rsqrt(var + GN_EPS) * lg_ref[...] + lb_ref[...]
    bonus = _segsum(r_ref[...] * k_ref[...] * rk_ref[...], ones) * v_ref[...]
    o_ref[...] = ((yn + bonus) * g_ref[...]).astype(o_ref.dtype)


def _post(y, rkv, km, v_src, g, lnx_g, lnx_b, r_k):
    n, d = y.shape
    bm = _blk(n, 512, SUBLANE)
    bn = _blk(d, 512, LANE)
    blk = pl.BlockSpec((bm, bn), lambda i, j: (i, j))
    prow = pl.BlockSpec((1, bn), lambda i, j: (0, j))
    v_arr, v_lead = v_src
    v_spec = blk if v_lead is None else pl.BlockSpec((None, bm, bn), lambda i, j: (v_lead, i, j))
    return pl.pallas_call(
        _post_kernel,
        grid=(n // bm, d // bn),
        in_specs=[blk, pl.BlockSpec((None, bm, bn), lambda i, j: (0, i, j)), blk, v_spec, blk,
                  prow, prow, prow],
        out_specs=blk,
        out_shape=jax.ShapeDtypeStruct((n, d), BF16),
        compiler_params=_cparams(("parallel", "parallel")),
        name="rwkv_post",
    )(y, rkv, km, v_arr, g, lnx_g.reshape(1, d), lnx_b.reshape(1, d), r_k.reshape(1, d))


def _pad_to(w, axis, mult):
    size = w.shape[axis]
    target = -(-size // mult) * mult
    if target == size:
        return w
    pad = [(0, 0)] * w.ndim
    pad[axis] = (0, target - size)
    return jnp.pad(w, pad)


def kernel(x, c, ada_w, ada_b, ada_emb, ln1_g, ln2_g, mlp_w1, mlp_w2, conv_w_in, conv_w, conv_w_out, rwkv_mu, rwkv_w_rkv, rwkv_w0, rwkv_w1, rwkv_w2, rwkv_a0, rwkv_a1, rwkv_a2, rwkv_g1, rwkv_g2, rwkv_k_k, rwkv_k_a, rwkv_r_k, rwkv_lnx_g, rwkv_lnx_b, rwkv_w_o, rwkv_v0, rwkv_v1, rwkv_v2, final_g):
    bsz, seq, d = x.shape
    depth = ada_emb.shape[0]
    n = bsz * seq
    assert d % GROUP == 0 and seq % CHUNK == 0

    mod = _ada(c, ada_w, ada_b, ada_emb)
    w1_b, w2_b = mlp_w1.astype(BF16), mlp_w2.astype(BF16)
    cin_b, cout_b = conv_w_in.astype(BF16), conv_w_out.astype(BF16)
    rkv_b, wo_b = rwkv_w_rkv.astype(BF16), rwkv_w_o.astype(BF16)
    xs = x.reshape(n, d)
    v_first = None
    for i in range(depth):
        j = i // 2
        if i % 2 == 0:
            h = _norm_mod(xs, ln1_g[i], mod, i, 1, 0, seq)
            gated = _conv_in(h, cin_b, conv_w, j, seq)
            xs = _mm_resid(gated, cout_b, j, xs, mod, i, 2, seq)
        else:
            use_v = v_first is not None
            downs = [rwkv_w1[j], rwkv_a1[j], rwkv_g1[j]]
            ups = [rwkv_w2[j], rwkv_a2[j], rwkv_g2[j]]
            if use_v:
                downs.append(rwkv_v1[j - 1])
                ups.append(rwkv_v2[j - 1])
            downs = [_pad_to(w, 1, LANE) for w in downs]
            ups = [_pad_to(w, 0, LANE) for w in ups]
            widths = tuple(w.shape[1] for w in downs) + ((0,) if not use_v else ())
            wd = jnp.concatenate(downs, axis=1).astype(BF16)
            wu = jnp.concatenate(ups, axis=0).astype(BF16)
            mixes, hid = _mix(xs, ln1_g[i], mod, i, rwkv_mu[j], wd, widths, use_v, seq)
            rkv = _rkv(mixes, rkv_b, j)
            outs = _prep(hid, wu, rkv, v_first, rwkv_w0[j], rwkv_a0[j], rwkv_k_k[j],
                         rwkv_k_a[j], rwkv_v0[j - 1] if use_v else None, widths, use_v)
            if use_v:
                lw, a, g, kkn, km, vm = outs
                v_src = (vm, None)
            else:
                lw, a, g, kkn, km = outs
                v_src = (rkv, 2)
                v_first = (rkv, 2)
            y = _wkv((rkv, 0), lw, km, v_src, kkn, a, bsz, seq)
            gated = _post(y, rkv, km, v_src, g, rwkv_lnx_g[j], rwkv_lnx_b[j], rwkv_r_k[j])
            xs = _mm_resid(gated, wo_b, j, xs, mod, i, 2, seq)
        h = _norm_mod(xs, ln2_g[i], mod, i, 4, 3, seq)
        hidden = _mlp_up(h, w1_b, i)
        xs = _mm_resid(hidden, w2_b, i, xs, mod, i, 5, seq)
    return _final_norm(xs, final_g).reshape(bsz, seq, d)
```

```python
import functools

import jax
import jax.numpy as jnp
from jax import lax
from jax.experimental import pallas as pl
from jax.experimental.pallas import tpu as pltpu

F32 = jnp.float32
BF16 = jnp.bfloat16

HEAD = 64
CHUNK = 64
SUB = 16
GROUP_HEADS = 4
GROUP = GROUP_HEADS * HEAD
WKV_GROUPS = 8
LANE = 128
SUBLANE = 8
ADA_CHUNKS = 6
CONV_WIDTH = 3
RMS_EPS = 1e-5
GN_EPS = 64e-5
VMEM_LIMIT = 56 * 1024 * 1024


def _blk(n, pref, align):
    best = None
    d = align
    while d <= min(n, pref):
        if n % d == 0:
            best = d
        d += align
    return best if best is not None else n


def _cparams(sem):
    return pltpu.CompilerParams(dimension_semantics=sem, vmem_limit_bytes=VMEM_LIMIT)


def _dot(a, b):
    return jnp.dot(a, b, preferred_element_type=F32)


def _dot_nt(a, b):
    return lax.dot_general(a, b, (((1,), (1,)), ((), ())), preferred_element_type=F32)


def _dot_tn(a, b):
    return lax.dot_general(a, b, (((0,), (0,)), ((), ())), preferred_element_type=F32)


def _split3(x):
    h1 = x.astype(BF16)
    r1 = x - h1.astype(F32)
    h2 = r1.astype(BF16)
    h3 = (r1 - h2.astype(F32)).astype(BF16)
    return h1, h2, h3


def _head_ones(n):
    r = lax.broadcasted_iota(jnp.int32, (n, n), 0) // HEAD
    c = lax.broadcasted_iota(jnp.int32, (n, n), 1) // HEAD
    return jnp.where(r == c, 1.0, 0.0).astype(BF16)


def _segsum(x, ones):
    outs = []
    for s in range(x.shape[1] // LANE):
        h1, h2, h3 = _split3(x[:, s * LANE:(s + 1) * LANE])
        outs.append(_dot(h1, ones) + _dot(h2, ones) + _dot(h3, ones))
    return outs[0] if len(outs) == 1 else jnp.concatenate(outs, axis=1)


def _ada_kernel(c_ref, w_ref, b_ref, emb_ref, o_ref, *, depth):
    s = jax.nn.silu(c_ref[...])
    acc = jnp.dot(s, w_ref[...], preferred_element_type=F32,
                  precision=lax.Precision.HIGHEST) + b_ref[...]
    for i in range(depth):
        o_ref[i] = acc + emb_ref[i]


def _ada(c, ada_w, ada_b, ada_emb):
    bsz, d = c.shape
    depth = ada_emb.shape[0]
    n = ada_w.shape[1]
    rows = -(-bsz // SUBLANE) * SUBLANE
    cp = jnp.pad(c, ((0, rows - bsz), (0, 0)))
    bn = _blk(n, 512, LANE)
    out = pl.pallas_call(
        functools.partial(_ada_kernel, depth=depth),
        grid=(n // bn,),
        in_specs=[
            pl.BlockSpec((rows, d), lambda j: (0, 0)),
            pl.BlockSpec((d, bn), lambda j: (0, j)),
            pl.BlockSpec((1, bn), lambda j: (0, j)),
            pl.BlockSpec((depth, 1, bn), lambda j: (0, 0, j)),
        ],
        out_specs=pl.BlockSpec((depth, rows, bn), lambda j: (0, 0, j)),
        out_shape=jax.ShapeDtypeStruct((depth, rows, n), F32),
        compiler_params=_cparams(("parallel",)),
        name="ada_mod",
    )(cp, ada_w, ada_b.reshape(1, n), ada_emb.reshape(depth, 1, n))
    mod = out[:, :bsz].reshape(depth, bsz, ADA_CHUNKS, d)
    return jnp.transpose(mod, (0, 2, 1, 3)).reshape(depth, ADA_CHUNKS, bsz, 1, d)


def _mod_spec(layer, chunk, tpb, bn, col_axis):
    if col_axis is None:
        return pl.BlockSpec((None, None, None, 1, bn), lambda i: (layer, chunk, i // tpb, 0, 0))
    return pl.BlockSpec((None, None, None, 1, bn),
                        lambda i, j, *_: (layer, chunk, i // tpb, 0, j))


def _rms(x, g):
    ms = jnp.mean(x * x, axis=-1, keepdims=True)
    return (x * lax.rsqrt(ms + RMS_EPS)) * g


def _norm_mod_kernel(x_ref, g_ref, sc_ref, sh_ref, o_ref):
    y = _rms(x_ref[...], g_ref[...])
    o_ref[...] = (y * (1.0 + sc_ref[...]) + sh_ref[...]).astype(o_ref.dtype)


def _norm_mod(x, g, mod, layer, sc_chunk, sh_chunk, seq):
    n, d = x.shape
    bm = _blk(seq, 256, SUBLANE)
    tpb = seq // bm
    return pl.pallas_call(
        _norm_mod_kernel,
        grid=(n // bm,),
        in_specs=[
            pl.BlockSpec((bm, d), lambda i: (i, 0)),
            pl.BlockSpec((1, d), lambda i: (0, 0)),
            _mod_spec(layer, sc_chunk, tpb, d, None),
            _mod_spec(layer, sh_chunk, tpb, d, None),
        ],
        out_specs=pl.BlockSpec((bm, d), lambda i: (i, 0)),
        out_shape=jax.ShapeDtypeStruct((n, d), BF16),
        compiler_params=_cparams(("parallel",)),
        name="norm_mod",
    )(x, g.reshape(1, d), mod, mod)


def _final_norm_kernel(x_ref, g_ref, o_ref):
    o_ref[...] = _rms(x_ref[...], g_ref[...])


def _final_norm(x, g):
    n, d = x.shape
    bm = _blk(n, 512, SUBLANE)
    return pl.pallas_call(
        _final_norm_kernel,
        grid=(n // bm,),
        in_specs=[pl.BlockSpec((bm, d), lambda i: (i, 0)),
                  pl.BlockSpec((1, d), lambda i: (0, 0))],
        out_specs=pl.BlockSpec((bm, d), lambda i: (i, 0)),
        out_shape=jax.ShapeDtypeStruct((n, d), F32),
        compiler_params=_cparams(("parallel",)),
        name="final_norm",
    )(x, g.reshape(1, d))


def _mix_kernel(x_ref, xp_ref, g_ref, sc_ref, sh_ref, mu_ref, wd_ref, mix_ref, hid_ref,
                *, tpb, widths, use_v):
    i = pl.program_id(0)

    def nm(x):
        return _rms(x, g_ref[...]) * (1.0 + sc_ref[...]) + sh_ref[...]

    h = nm(x_ref[...])
    hp = nm(xp_ref[...])[SUBLANE - 1:SUBLANE]
    hp = jnp.where(i % tpb == 0, 0.0, hp)
    rows = lax.broadcasted_iota(jnp.int32, h.shape, 0)
    xx = jnp.where(rows == 0, hp, pltpu.roll(h, 1, 0)) - h

    def mix(j):
        return (h + xx * mu_ref[j:j + 1, :]).astype(BF16)

    mix_ref[0] = mix(0)
    mix_ref[1] = mix(1)
    xv = mix(2)
    mix_ref[2] = xv
    wl, al, gl, vl = widths
    o = 0
    hid_ref[:, o:o + wl] = jnp.tanh(_dot(mix(3), wd_ref[:, o:o + wl])).astype(BF16)
    o += wl
    hid_ref[:, o:o + al] = _dot(mix(4), wd_ref[:, o:o + al]).astype(BF16)
    o += al
    hid_ref[:, o:o + gl] = jax.nn.sigmoid(_dot(mix(5), wd_ref[:, o:o + gl])).astype(BF16)
    o += gl
    if use_v:
        hid_ref[:, o:o + vl] = _dot(xv, wd_ref[:, o:o + vl]).astype(BF16)


def _mix(x, g, mod, layer, mu, wd, widths, use_v, seq):
    n, d = x.shape
    hl = wd.shape[1]
    bm = _blk(seq, 128, SUBLANE)
    tpb = seq // bm
    per8 = bm // SUBLANE
    return pl.pallas_call(
        functools.partial(_mix_kernel, tpb=tpb, widths=widths, use_v=use_v),
        grid=(n // bm,),
        in_specs=[
            pl.BlockSpec((bm, d), lambda i: (i, 0)),
            pl.BlockSpec((SUBLANE, d), lambda i: (jnp.maximum(i * per8 - 1, 0), 0)),
            pl.BlockSpec((1, d), lambda i: (0, 0)),
            _mod_spec(layer, 1, tpb, d, None),
            _mod_spec(layer, 0, tpb, d, None),
            pl.BlockSpec((6, d), lambda i: (0, 0)),
            pl.BlockSpec((d, hl), lambda i: (0, 0)),
        ],
        out_specs=[pl.BlockSpec((3, bm, d), lambda i: (0, i, 0)),
                   pl.BlockSpec((bm, hl), lambda i: (i, 0))],
        out_shape=[jax.ShapeDtypeStruct((3, n, d), BF16),
                   jax.ShapeDtypeStruct((n, hl), BF16)],
        compiler_params=_cparams(("parallel",)),
        name="rwkv_mix",
    )(x, x, g.reshape(1, d), mod, mod, mu, wd)


def _conv_in_kernel(h_ref, wb_ref, wc_ref, wu_ref, cw_ref, o_ref, carry_ref, *, tpb):
    m = pl.program_id(1)
    h = h_ref[...]
    bg = _dot(h, wb_ref[...])
    z = _dot(h, wc_ref[...]) * _dot(h, wu_ref[...])
    bm = z.shape[0]
    prev = jnp.where(m % tpb == 0, 0.0, carry_ref[...])
    carry_ref[...] = z[bm - SUBLANE:bm]
    w0 = cw_ref[0:1, :]
    w1 = cw_ref[1:2, :]
    w2 = cw_ref[2:3, :]
    y = w0 * pltpu.roll(z, 2, 0) + w1 * pltpu.roll(z, 1, 0) + w2 * z
    o_ref[...] = (bg * y).astype(o_ref.dtype)
    zt = z[0:SUBLANE]
    r8 = lax.broadcasted_iota(jnp.int32, zt.shape, 0)
    z1 = jnp.where(r8 < 1, pltpu.roll(prev, 1, 0), pltpu.roll(zt, 1, 0))
    z2 = jnp.where(r8 < 2, pltpu.roll(prev, 2, 0), pltpu.roll(zt, 2, 0))
    yt = w0 * z2 + w1 * z1 + w2 * zt
    o_ref[0:SUBLANE, :] = (bg[0:SUBLANE] * yt).astype(o_ref.dtype)


def _conv_in(h, w_in, conv_w, layer, seq):
    n, d = h.shape
    bm = _blk(seq, 1024, SUBLANE)
    bn = _blk(d, 256, LANE)
    tpb = seq // bm
    nj = d // bn
    return pl.pallas_call(
        functools.partial(_conv_in_kernel, tpb=tpb),
        grid=(nj, n // bm),
        in_specs=[
            pl.BlockSpec((bm, d), lambda j, i: (i, 0)),
            pl.BlockSpec((None, d, bn), lambda j, i: (layer, 0, j)),
            pl.BlockSpec((None, d, bn), lambda j, i: (layer, 0, j + nj)),
            pl.BlockSpec((None, d, bn), lambda j, i: (layer, 0, j + 2 * nj)),
            pl.BlockSpec((None, CONV_WIDTH, bn), lambda j, i: (layer, 0, j)),
        ],
        out_specs=pl.BlockSpec((bm, bn), lambda j, i: (i, j)),
        out_shape=jax.ShapeDtypeStruct((n, d), BF16),
        scratch_shapes=[pltpu.VMEM((SUBLANE, bn), F32)],
        compiler_params=_cparams(("arbitrary", "arbitrary")),
        name="conv_in",
    )(h, w_in, w_in, w_in, conv_w)


def _mm_resid_kernel(a_ref, w_ref, x_ref, gate_ref, o_ref, *, nk):
    if nk == 1:
        o_ref[...] = x_ref[...] + gate_ref[...] * _dot(a_ref[...], w_ref[...])
        return
    k = pl.program_id(2)

    @pl.when(k == 0)
    def _():
        o_ref[...] = jnp.zeros_like(o_ref)

    acc = o_ref[...] + _dot(a_ref[...], w_ref[...])
    o_ref[...] = jnp.where(k == nk - 1, x_ref[...] + gate_ref[...] * acc, acc)


def _mm_resid(a, w, widx, x, mod, layer, gate_chunk, seq):
    n, kdim = a.shape
    d = w.shape[2]
    bm = _blk(seq, 1024, SUBLANE)
    bk = _blk(kdim, 4096 if kdim <= 4096 else 2048, LANE)
    nk = kdim // bk
    bn = _blk(d, 512 if nk == 1 else 1024, LANE)
    tpb = seq // bm
    return pl.pallas_call(
        functools.partial(_mm_resid_kernel, nk=nk),
        grid=(n // bm, d // bn, nk),
        in_specs=[
            pl.BlockSpec((bm, bk), lambda i, j, k: (i, k)),
            pl.BlockSpec((None, bk, bn), lambda i, j, k: (widx, k, j)),
            pl.BlockSpec((bm, bn), lambda i, j, k: (i, j)),
            _mod_spec(layer, gate_chunk, tpb, bn, 1),
        ],
        out_specs=pl.BlockSpec((bm, bn), lambda i, j, k: (i, j)),
        out_shape=jax.ShapeDtypeStruct((n, d), F32),
        compiler_params=_cparams(("parallel", "parallel", "arbitrary")),
        name="mm_resid",
    )(a, w, x, mod)


def _mlp_up_kernel(a_ref, w_ref, o_ref):
    o_ref[...] = jnp.square(jnp.maximum(_dot(a_ref[...], w_ref[...]), 0.0)).astype(o_ref.dtype)


def _mlp_up(a, w, widx):
    n, kdim = a.shape
    f = w.shape[2]
    bm = _blk(n, 1024, SUBLANE)
    bn = _blk(f, 1024, LANE)
    return pl.pallas_call(
        _mlp_up_kernel,
        grid=(n // bm, f // bn),
        in_specs=[pl.BlockSpec((bm, kdim), lambda i, j: (i, 0)),
                  pl.BlockSpec((None, kdim, bn), lambda i, j: (widx, 0, j))],
        out_specs=pl.BlockSpec((bm, bn), lambda i, j: (i, j)),
        out_shape=jax.ShapeDtypeStruct((n, f), BF16),
        compiler_params=_cparams(("parallel", "parallel")),
        name="mlp_up",
    )(a, w)


def _rkv_kernel(a_ref, w_ref, o_ref):
    o_ref[...] = _dot(a_ref[...], w_ref[...])


def _rkv(mixes, w_rkv, widx):
    _, n, d = mixes.shape
    bm = _blk(n, 1024, SUBLANE)
    bn = _blk(d, 1024, LANE)
    return pl.pallas_call(
        _rkv_kernel,
        grid=(3, n // bm, d // bn),
        in_specs=[pl.BlockSpec((None, bm, d), lambda s, i, j: (s, i, 0)),
                  pl.BlockSpec((None, None, d, bn), lambda s, i, j: (widx, s, 0, j))],
        out_specs=pl.BlockSpec((None, bm, bn), lambda s, i, j: (s, i, j)),
        out_shape=jax.ShapeDtypeStruct((3, n, d), F32),
        compiler_params=_cparams(("parallel", "parallel", "parallel")),
        name="rkv_proj",
    )(mixes, w_rkv)


def _prep_kernel(*refs, widths, use_v):
    if use_v:
        (hid_ref, wu_ref, k_ref, v_ref, vf_ref, w0_ref, a0_ref, kk_ref, ka_ref, v0_ref,
         lw_ref, a_ref, g_ref, kkn_ref, km_ref, vm_ref) = refs
    else:
        (hid_ref, wu_ref, k_ref, w0_ref, a0_ref, kk_ref, ka_ref,
         lw_ref, a_ref, g_ref, kkn_ref, km_ref) = refs
    wl, al, gl, vl = widths
    o = 0
    x = w0_ref[...] + _dot(hid_ref[:, o:o + wl], wu_ref[o:o + wl, :])
    o += wl
    sp = jnp.maximum(-x, 0.0) + jnp.log(1.0 + jnp.exp(-jnp.abs(x)))
    lw_ref[...] = -jnp.exp(-sp - 0.5)
    a = jax.nn.sigmoid(a0_ref[...] + _dot(hid_ref[:, o:o + al], wu_ref[o:o + al, :]))
    o += al
    a_ref[...] = a
    g_ref[...] = _dot(hid_ref[:, o:o + gl], wu_ref[o:o + gl, :])
    o += gl
    k = k_ref[...]
    kk = k * kk_ref[...]
    ss = _segsum(kk * kk, _head_ones(LANE))
    kkn_ref[...] = kk * lax.rsqrt(jnp.maximum(ss, 1e-24))
    km_ref[...] = k * (1.0 + (a - 1.0) * ka_ref[...])
    if use_v:
        v = v_ref[...]
        vg = jax.nn.sigmoid(v0_ref[...] + _dot(hid_ref[:, o:o + vl], wu_ref[o:o + vl, :]))
        vm_ref[...] = v + (vf_ref[...] - v) * vg


def _prep(hid, wu, rkv, v_first, w0, a0, k_k, k_a, v0, widths, use_v):
    n, hl = hid.shape
    d = wu.shape[1]
    bm = _blk(n, 512, SUBLANE)
    bn = _blk(d, 512, LANE)
    row = lambda p: p.reshape(1, d)
    blk = pl.BlockSpec((bm, bn), lambda i, j: (i, j))
    prow = pl.BlockSpec((1, bn), lambda i, j: (0, j))
    ins = [hid, wu, rkv]
    specs = [pl.BlockSpec((bm, hl), lambda i, j: (i, 0)),
             pl.BlockSpec((hl, bn), lambda i, j: (0, j)),
             pl.BlockSpec((None, bm, bn), lambda i, j: (1, i, j))]
    if use_v:
        vf_arr, vf_lead = v_first
        ins += [rkv, vf_arr]
        specs += [pl.BlockSpec((None, bm, bn), lambda i, j: (2, i, j)),
                  pl.BlockSpec((None, bm, bn), lambda i, j: (vf_lead, i, j))]
    ins += [row(w0), row(a0), row(k_k), row(k_a)]
    specs += [prow] * 4
    n_out = 5
    if use_v:
        ins.append(row(v0))
        specs.append(prow)
        n_out = 6
    return pl.pallas_call(
        functools.partial(_prep_kernel, widths=widths, use_v=use_v),
        grid=(n // bm, d // bn),
        in_specs=specs,
        out_specs=[blk] * n_out,
        out_shape=[jax.ShapeDtypeStruct((n, d), F32)] * n_out,
        compiler_params=_cparams(("parallel", "parallel")),
        name="rwkv_prep",
    )(*ins)


def _wkv_chunk(s_prev, ins, consts):
    tri, head_masks, strict_w, incl_w, blk_w, same, eye = consts
    ng = range(len(ins))

    def stack(x):
        return jnp.concatenate([jnp.where(m, x, 0.0) for m in head_masks], axis=0).astype(BF16)

    def bdot(x, y):
        return _dot(x, y).astype(BF16)

    ar, kb_s, v_s, vk_l, vk_r, gl = [], [], [], [], [], []
    for r, lw, k, v, kk, a in ins:
        h1, h2, h3 = _split3(lw)
        cum = _dot(tri, h1) + _dot(tri, h2) + _dot(tri, h3)
        cl = cum[CHUNK - 1:CHUNK]
        gi = jnp.exp(-cum)
        b = kk * a
        dl = jnp.exp(cl - cum)
        ar.append(jnp.concatenate([-(kk * jnp.exp(cum - lw)), r * jnp.exp(cum)], axis=0).astype(BF16))
        kb_s.append(jnp.concatenate([stack(k * gi), stack(b * gi)], axis=0))
        v_s.append(stack(v))
        vk_l.append(v.astype(BF16))
        vk_r.append(jnp.concatenate([k * dl, b * dl], axis=0).astype(BF16))
        gl.append(jnp.exp(cl))

    sc = [_dot_nt(ar[g], kb_s[g]) for g in ng]
    a_ak = [jnp.where(strict_w, sc[g][0:CHUNK, 0:GROUP], 0.0).astype(BF16) for g in ng]
    a_rk = [jnp.where(incl_w, sc[g][CHUNK:2 * CHUNK, 0:GROUP], 0.0).astype(BF16) for g in ng]
    a_rb = [jnp.where(incl_w, sc[g][CHUNK:2 * CHUNK, GROUP:2 * GROUP], 0.0).astype(BF16) for g in ng]
    n_w = [jnp.where(strict_w, sc[g][0:CHUNK, GROUP:2 * GROUP], 0.0) for g in ng]
    dm = [stack(jnp.where(blk_w, n_w[g], 0.0)) for g in ng]
    e = [stack(jnp.where(blk_w, 0.0, n_w[g])) for g in ng]

    ss = [_dot_nt(ar[g], s_prev[g].astype(BF16)) for g in ng]
    pm = [stack(ss[g][0:CHUNK] + _dot(a_ak[g], v_s[g])) for g in ng]
    y0 = [ss[g][CHUNK:2 * CHUNK] + _dot(a_rk[g], v_s[g]) for g in ng]

    p = dm
    td = [eye + dm[g] for g in ng]
    for _ in range(3):
        p = [bdot(p[g], p[g]) for g in ng]
        td = [(td[g].astype(F32) + _dot(td[g], p[g])).astype(BF16) for g in ng]
    m1 = [bdot(td[g], e[g]) for g in ng]
    u = [bdot(td[g], pm[g]) for g in ng]
    m2 = [bdot(m1[g], m1[g]) for g in ng]
    w = [(u[g].astype(F32) + _dot(m2[g], u[g])).astype(BF16) for g in ng]
    sa = [(w[g].astype(F32) + _dot(m1[g], w[g])).astype(BF16) for g in ng]

    y, s_next = [], []
    for g in ng:
        y.append(y0[g] + _dot(a_rb[g], sa[g]))
    for g in ng:
        sa_n = sa[g][0:CHUNK]
        for hh in range(1, GROUP_HEADS):
            sa_n = sa_n + sa[g][hh * CHUNK:(hh + 1) * CHUNK]
        lhs = jnp.concatenate([vk_l[g], sa_n], axis=0)
        upd = _dot_tn(lhs, vk_r[g])
        s_next.append(s_prev[g] * gl[g] + jnp.where(same, upd, 0.0))
    return y, s_next


def _wkv_kernel(r_ref, lw_ref, k_ref, v_ref, kk_ref, a_ref, y_ref, s_ref, *, groups):
    c = pl.program_id(2)

    @pl.when(c == 0)
    def _():
        s_ref[...] = jnp.zeros_like(s_ref)

    ri = lax.broadcasted_iota(jnp.int32, (GROUP, GROUP), 0)
    ci = lax.broadcasted_iota(jnp.int32, (GROUP, GROUP), 1)
    same = (ri // CHUNK) == (ci // CHUNK)
    eye = jnp.where(ri == ci, 1.0, 0.0).astype(BF16)
    wt = lax.broadcasted_iota(jnp.int32, (CHUNK, GROUP), 0)
    ws = lax.broadcasted_iota(jnp.int32, (CHUNK, GROUP), 1) % CHUNK
    strict_w = wt > ws
    incl_w = wt >= ws
    blk_w = (wt // SUB) == (ws // SUB)
    ti = lax.broadcasted_iota(jnp.int32, (CHUNK, CHUNK), 0)
    tj = lax.broadcasted_iota(jnp.int32, (CHUNK, CHUNK), 1)
    tri = jnp.where(ti >= tj, 1.0, 0.0).astype(BF16)
    lane_head = lax.broadcasted_iota(jnp.int32, (CHUNK, GROUP), 1) // HEAD
    head_masks = [lane_head == hh for hh in range(GROUP_HEADS)]
    consts = (tri, head_masks, strict_w, incl_w, blk_w, same, eye)

    cols = [slice(g * GROUP, (g + 1) * GROUP) for g in range(groups)]
    ins = [tuple(ref[:, sl] for ref in (r_ref, lw_ref, k_ref, v_ref, kk_ref, a_ref)) for sl in cols]
    y, s_next = _wkv_chunk([s_ref[g] for g in range(groups)], ins, consts)
    for g, sl in enumerate(cols):
        y_ref[:, sl] = y[g]
        s_ref[g] = s_next[g]


def _wkv(r_src, lw, k, v_src, kk, a, bsz, seq):
    n, d = lw.shape
    groups = _blk(d, WKV_GROUPS * GROUP, GROUP) // GROUP
    gw = groups * GROUP
    nc = seq // CHUNK

    def spec(src):
        arr, lead = src
        if lead is None:
            return arr, pl.BlockSpec((CHUNK, gw), lambda bb, j, c: (bb * nc + c, j))
        return arr, pl.BlockSpec((None, CHUNK, gw), lambda bb, j, c: (lead, bb * nc + c, j))

    ins, specs = zip(*[spec(s) for s in (r_src, (lw, None), (k, None), v_src, (kk, None), (a, None))])
    return pl.pallas_call(
        functools.partial(_wkv_kernel, groups=groups),
        grid=(bsz, d // gw, nc),
        in_specs=list(specs),
        out_specs=pl.BlockSpec((CHUNK, gw), lambda bb, j, c: (bb * nc + c, j)),
        out_shape=jax.ShapeDtypeStruct((n, d), F32),
        scratch_shapes=[pltpu.VMEM((groups, GROUP, GROUP), F32)],
        compiler_params=_cparams(("parallel", "parallel", "arbitrary")),
        name="wkv7",
    )(*ins)


def _post_kernel(y_ref, r_ref, k_ref, v_ref, g_ref, lg_ref, lb_ref, rk_ref, o_ref):
    ones = _head_ones(LANE)
    y = y_ref[...]
    inv = 1.0 / HEAD
    dlt = y - _segsum(y, ones) * inv
    var = _segsum(dlt * dlt, ones) * inv
    yn = dlt * lax.rsqrt(var + GN_EPS) * lg_ref[...] + lb_ref[...]
    bonus = _segsum(r_ref[...] * k_ref[...] * rk_ref[...], ones) * v_ref[...]
    o_ref[...] = ((yn + bonus) * g_ref[...]).astype(o_ref.dtype)


def _post(y, rkv, km, v_src, g, lnx_g, lnx_b, r_k):
    n, d = y.shape
    bm = _blk(n, 512, SUBLANE)
    bn = _blk(d, 512, LANE)
    blk = pl.BlockSpec((bm, bn), lambda i, j: (i, j))
    prow = pl.BlockSpec((1, bn), lambda i, j: (0, j))
    v_arr, v_lead = v_src
    v_spec = blk if v_lead is None else pl.BlockSpec((None, bm, bn), lambda i, j: (v_lead, i, j))
    return pl.pallas_call(
        _post_kernel,
        grid=(n // bm, d // bn),
        in_specs=[blk, pl.BlockSpec((None, bm, bn), lambda i, j: (0, i, j)), blk, v_spec, blk,
                  prow, prow, prow],
        out_specs=blk,
        out_shape=jax.ShapeDtypeStruct((n, d), BF16),
        compiler_params=_cparams(("parallel", "parallel")),
        name="rwkv_post",
    )(y, rkv, km, v_arr, g, lnx_g.reshape(1, d), lnx_b.reshape(1, d), r_k.reshape(1, d))


def _pad_to(w, axis, mult):
    size = w.shape[axis]
    target = -(-size // mult) * mult
    if target == size:
        return w
    pad = [(0, 0)] * w.ndim
    pad[axis] = (0, target - size)
    return jnp.pad(w, pad)


def kernel(x, c, ada_w, ada_b, ada_emb, ln1_g, ln2_g, mlp_w1, mlp_w2, conv_w_in, conv_w, conv_w_out, rwkv_mu, rwkv_w_rkv, rwkv_w0, rwkv_w1, rwkv_w2, rwkv_a0, rwkv_a1, rwkv_a2, rwkv_g1, rwkv_g2, rwkv_k_k, rwkv_k_a, rwkv_r_k, rwkv_lnx_g, rwkv_lnx_b, rwkv_w_o, rwkv_v0, rwkv_v1, rwkv_v2, final_g):
    bsz, seq, d = x.shape
    depth = ada_emb.shape[0]
    n = bsz * seq
    assert d % GROUP == 0 and seq % CHUNK == 0

    mod = _ada(c, ada_w, ada_b, ada_emb)
    w1_b, w2_b = mlp_w1.astype(BF16), mlp_w2.astype(BF16)
    cin_b, cout_b = conv_w_in.astype(BF16), conv_w_out.astype(BF16)
    rkv_b, wo_b = rwkv_w_rkv.astype(BF16), rwkv_w_o.astype(BF16)
    xs = x.reshape(n, d)
    v_first = None
    for i in range(depth):
        j = i // 2
        if i % 2 == 0:
            h = _norm_mod(xs, ln1_g[i], mod, i, 1, 0, seq)
            gated = _conv_in(h, cin_b, conv_w, j, seq)
            xs = _mm_resid(gated, cout_b, j, xs, mod, i, 2, seq)
        else:
            use_v = v_first is not None
            downs = [rwkv_w1[j], rwkv_a1[j], rwkv_g1[j]]
            ups = [rwkv_w2[j], rwkv_a2[j], rwkv_g2[j]]
            if use_v:
                downs.append(rwkv_v1[j - 1])
                ups.append(rwkv_v2[j - 1])
            downs = [_pad_to(w, 1, LANE) for w in downs]
            ups = [_pad_to(w, 0, LANE) for w in ups]
            widths = tuple(w.shape[1] for w in downs) + ((0,) if not use_v else ())
            wd = jnp.concatenate(downs, axis=1).astype(BF16)
            wu = jnp.concatenate(ups, axis=0).astype(BF16)
            mixes, hid = _mix(xs, ln1_g[i], mod, i, rwkv_mu[j], wd, widths, use_v, seq)
            rkv = _rkv(mixes, rkv_b, j)
            outs = _prep(hid, wu, rkv, v_first, rwkv_w0[j], rwkv_a0[j], rwkv_k_k[j],
                         rwkv_k_a[j], rwkv_v0[j - 1] if use_v else None, widths, use_v)
            if use_v:
                lw, a, g, kkn, km, vm = outs
                v_src = (vm, None)
            else:
                lw, a, g, kkn, km = outs
                v_src = (rkv, 2)
                v_first = (rkv, 2)
            y = _wkv((rkv, 0), lw, km, v_src, kkn, a, bsz, seq)
            gated = _post(y, rkv, km, v_src, g, rwkv_lnx_g[j], rwkv_lnx_b[j], rwkv_r_k[j])
            xs = _mm_resid(gated, wo_b, j, xs, mod, i, 2, seq)
        h = _norm_mod(xs, ln2_g[i], mod, i, 4, 3, seq)
        hidden = _mlp_up(h, w1_b, i)
        xs = _mm_resid(hidden, w2_b, i, xs, mod, i, 5, seq)
    return _final_norm(xs, final_g).reshape(bsz, seq, d)
```

```python
import functools

import jax
import jax.numpy as jnp
from jax import lax
from jax.experimental import pallas as pl
from jax.experimental.pallas import tpu as pltpu

F32 = jnp.float32
BF16 = jnp.bfloat16

HEAD = 64
CHUNK = 64
SUB = 16
GROUP_HEADS = 4
GROUP = GROUP_HEADS * HEAD
WKV_GROUPS = 16
LANE = 128
SUBLANE = 8
ADA_CHUNKS = 6
CONV_WIDTH = 3
RMS_EPS = 1e-5
GN_EPS = 64e-5
VMEM_LIMIT = 56 * 1024 * 1024


def _blk(n, pref, align):
    best = None
    d = align
    while d <= min(n, pref):
        if n % d == 0:
            best = d
        d += align
    return best if best is not None else n


def _cparams(sem):
    return pltpu.CompilerParams(dimension_semantics=sem, vmem_limit_bytes=VMEM_LIMIT)


def _dot(a, b):
    return jnp.dot(a, b, preferred_element_type=F32)


def _dot_nt(a, b):
    return lax.dot_general(a, b, (((1,), (1,)), ((), ())), preferred_element_type=F32)


def _dot_tn(a, b):
    return lax.dot_general(a, b, (((0,), (0,)), ((), ())), preferred_element_type=F32)


def _split3(x):
    h1 = x.astype(BF16)
    r1 = x - h1.astype(F32)
    h2 = r1.astype(BF16)
    h3 = (r1 - h2.astype(F32)).astype(BF16)
    return h1, h2, h3


def _head_ones(n):
    r = lax.broadcasted_iota(jnp.int32, (n, n), 0) // HEAD
    c = lax.broadcasted_iota(jnp.int32, (n, n), 1) // HEAD
    return jnp.where(r == c, 1.0, 0.0).astype(BF16)


def _segsum(x, ones):
    outs = []
    for s in range(x.shape[1] // LANE):
        h1, h2, h3 = _split3(x[:, s * LANE:(s + 1) * LANE])
        outs.append(_dot(h1, ones) + _dot(h2, ones) + _dot(h3, ones))
    return outs[0] if len(outs) == 1 else jnp.concatenate(outs, axis=1)


def _ada_kernel(c_ref, w_ref, b_ref, emb_ref, o_ref, *, depth):
    s = jax.nn.silu(c_ref[...])
    acc = jnp.dot(s, w_ref[...], preferred_element_type=F32,
                  precision=lax.Precision.HIGHEST) + b_ref[...]
    for i in range(depth):
        o_ref[i] = acc + emb_ref[i]


def _ada(c, ada_w, ada_b, ada_emb):
    bsz, d = c.shape
    depth = ada_emb.shape[0]
    n = ada_w.shape[1]
    rows = -(-bsz // SUBLANE) * SUBLANE
    cp = jnp.pad(c, ((0, rows - bsz), (0, 0)))
    bn = _blk(n, 512, LANE)
    out = pl.pallas_call(
        functools.partial(_ada_kernel, depth=depth),
        grid=(n // bn,),
        in_specs=[
            pl.BlockSpec((rows, d), lambda j: (0, 0)),
            pl.BlockSpec((d, bn), lambda j: (0, j)),
            pl.BlockSpec((1, bn), lambda j: (0, j)),
            pl.BlockSpec((depth, 1, bn), lambda j: (0, 0, j)),
        ],
        out_specs=pl.BlockSpec((depth, rows, bn), lambda j: (0, 0, j)),
        out_shape=jax.ShapeDtypeStruct((depth, rows, n), F32),
        compiler_params=_cparams(("parallel",)),
        name="ada_mod",
    )(cp, ada_w, ada_b.reshape(1, n), ada_emb.reshape(depth, 1, n))
    mod = out[:, :bsz].reshape(depth, bsz, ADA_CHUNKS, d)
    return jnp.transpose(mod, (0, 2, 1, 3)).reshape(depth, ADA_CHUNKS, bsz, 1, d)


def _mod_spec(layer, chunk, tpb, bn, col_axis):
    if col_axis is None:
        return pl.BlockSpec((None, None, None, 1, bn), lambda i: (layer, chunk, i // tpb, 0, 0))
    return pl.BlockSpec((None, None, None, 1, bn),
                        lambda i, j, *_: (layer, chunk, i // tpb, 0, j))


def _rms(x, g):
    ms = jnp.mean(x * x, axis=-1, keepdims=True)
    return (x * lax.rsqrt(ms + RMS_EPS)) * g


def _norm_mod_kernel(x_ref, g_ref, sc_ref, sh_ref, o_ref):
    y = _rms(x_ref[...], g_ref[...])
    o_ref[...] = (y * (1.0 + sc_ref[...]) + sh_ref[...]).astype(o_ref.dtype)


def _norm_mod(x, g, mod, layer, sc_chunk, sh_chunk, seq):
    n, d = x.shape
    bm = _blk(seq, 256, SUBLANE)
    tpb = seq // bm
    return pl.pallas_call(
        _norm_mod_kernel,
        grid=(n // bm,),
        in_specs=[
            pl.BlockSpec((bm, d), lambda i: (i, 0)),
            pl.BlockSpec((1, d), lambda i: (0, 0)),
            _mod_spec(layer, sc_chunk, tpb, d, None),
            _mod_spec(layer, sh_chunk, tpb, d, None),
        ],
        out_specs=pl.BlockSpec((bm, d), lambda i: (i, 0)),
        out_shape=jax.ShapeDtypeStruct((n, d), BF16),
        compiler_params=_cparams(("parallel",)),
        name="norm_mod",
    )(x, g.reshape(1, d), mod, mod)


def _final_norm_kernel(x_ref, g_ref, o_ref):
    o_ref[...] = _rms(x_ref[...], g_ref[...])


def _final_norm(x, g):
    n, d = x.shape
    bm = _blk(n, 512, SUBLANE)
    return pl.pallas_call(
        _final_norm_kernel,
        grid=(n // bm,),
        in_specs=[pl.BlockSpec((bm, d), lambda i: (i, 0)),
                  pl.BlockSpec((1, d), lambda i: (0, 0))],
        out_specs=pl.BlockSpec((bm, d), lambda i: (i, 0)),
        out_shape=jax.ShapeDtypeStruct((n, d), F32),
        compiler_params=_cparams(("parallel",)),
        name="final_norm",
    )(x, g.reshape(1, d))


def _mix_kernel(x_ref, xp_ref, g_ref, sc_ref, sh_ref, mu_ref, wd_ref, mix_ref, hid_ref,
                *, tpb, widths, use_v):
    i = pl.program_id(0)

    def nm(x):
        return _rms(x, g_ref[...]) * (1.0 + sc_ref[...]) + sh_ref[...]

    h = nm(x_ref[...])
    hp = nm(xp_ref[...])[SUBLANE - 1:SUBLANE]
    hp = jnp.where(i % tpb == 0, 0.0, hp)
    rows = lax.broadcasted_iota(jnp.int32, h.shape, 0)
    xx = jnp.where(rows == 0, hp, pltpu.roll(h, 1, 0)) - h

    def mix(j):
        return (h + xx * mu_ref[j:j + 1, :]).astype(BF16)

    mix_ref[0] = mix(0)
    mix_ref[1] = mix(1)
    xv = mix(2)
    mix_ref[2] = xv
    wl, al, gl, vl = widths
    o = 0
    hid_ref[:, o:o + wl] = jnp.tanh(_dot(mix(3), wd_ref[:, o:o + wl])).astype(BF16)
    o += wl
    hid_ref[:, o:o + al] = _dot(mix(4), wd_ref[:, o:o + al]).astype(BF16)
    o += al
    hid_ref[:, o:o + gl] = jax.nn.sigmoid(_dot(mix(5), wd_ref[:, o:o + gl])).astype(BF16)
    o += gl
    if use_v:
        hid_ref[:, o:o + vl] = _dot(xv, wd_ref[:, o:o + vl]).astype(BF16)


def _mix(x, g, mod, layer, mu, wd, widths, use_v, seq):
    n, d = x.shape
    hl = wd.shape[1]
    bm = _blk(seq, 128, SUBLANE)
    tpb = seq // bm
    per8 = bm // SUBLANE
    return pl.pallas_call(
        functools.partial(_mix_kernel, tpb=tpb, widths=widths, use_v=use_v),
        grid=(n // bm,),
        in_specs=[
            pl.BlockSpec((bm, d), lambda i: (i, 0)),
            pl.BlockSpec((SUBLANE, d), lambda i: (jnp.maximum(i * per8 - 1, 0), 0)),
            pl.BlockSpec((1, d), lambda i: (0, 0)),
            _mod_spec(layer, 1, tpb, d, None),
            _mod_spec(layer, 0, tpb, d, None),
            pl.BlockSpec((6, d), lambda i: (0, 0)),
            pl.BlockSpec((d, hl), lambda i: (0, 0)),
        ],
        out_specs=[pl.BlockSpec((3, bm, d), lambda i: (0, i, 0)),
                   pl.BlockSpec((bm, hl), lambda i: (i, 0))],
        out_shape=[jax.ShapeDtypeStruct((3, n, d), BF16),
                   jax.ShapeDtypeStruct((n, hl), BF16)],
        compiler_params=_cparams(("parallel",)),
        name="rwkv_mix",
    )(x, x, g.reshape(1, d), mod, mod, mu, wd)


def _conv_in_kernel(h_ref, wb_ref, wc_ref, wu_ref, cw_ref, o_ref, carry_ref, *, tpb):
    m = pl.program_id(1)
    h = h_ref[...]
    bg = _dot(h, wb_ref[...])
    z = _dot(h, wc_ref[...]) * _dot(h, wu_ref[...])
    bm = z.shape[0]
    prev = jnp.where(m % tpb == 0, 0.0, carry_ref[...])
    carry_ref[...] = z[bm - SUBLANE:bm]
    w0 = cw_ref[0:1, :]
    w1 = cw_ref[1:2, :]
    w2 = cw_ref[2:3, :]
    y = w0 * pltpu.roll(z, 2, 0) + w1 * pltpu.roll(z, 1, 0) + w2 * z
    o_ref[...] = (bg * y).astype(o_ref.dtype)
    zt = z[0:SUBLANE]
    r8 = lax.broadcasted_iota(jnp.int32, zt.shape, 0)
    z1 = jnp.where(r8 < 1, pltpu.roll(prev, 1, 0), pltpu.roll(zt, 1, 0))
    z2 = jnp.where(r8 < 2, pltpu.roll(prev, 2, 0), pltpu.roll(zt, 2, 0))
    yt = w0 * z2 + w1 * z1 + w2 * zt
    o_ref[0:SUBLANE, :] = (bg[0:SUBLANE] * yt).astype(o_ref.dtype)


def _conv_in(h, w_in, conv_w, layer, seq):
    n, d = h.shape
    bm = _blk(seq, 1024, SUBLANE)
    bn = _blk(d, 256, LANE)
    tpb = seq // bm
    nj = d // bn
    return pl.pallas_call(
        functools.partial(_conv_in_kernel, tpb=tpb),
        grid=(nj, n // bm),
        in_specs=[
            pl.BlockSpec((bm, d), lambda j, i: (i, 0)),
            pl.BlockSpec((None, d, bn), lambda j, i: (layer, 0, j)),
            pl.BlockSpec((None, d, bn), lambda j, i: (layer, 0, j + nj)),
            pl.BlockSpec((None, d, bn), lambda j, i: (layer, 0, j + 2 * nj)),
            pl.BlockSpec((None, CONV_WIDTH, bn), lambda j, i: (layer, 0, j)),
        ],
        out_specs=pl.BlockSpec((bm, bn), lambda j, i: (i, j)),
        out_shape=jax.ShapeDtypeStruct((n, d), BF16),
        scratch_shapes=[pltpu.VMEM((SUBLANE, bn), F32)],
        compiler_params=_cparams(("arbitrary", "arbitrary")),
        name="conv_in",
    )(h, w_in, w_in, w_in, conv_w)


def _mm_resid_kernel(a_ref, w_ref, x_ref, gate_ref, o_ref, *, nk):
    if nk == 1:
        o_ref[...] = x_ref[...] + gate_ref[...] * _dot(a_ref[...], w_ref[...])
        return
    k = pl.program_id(2)

    @pl.when(k == 0)
    def _():
        o_ref[...] = jnp.zeros_like(o_ref)

    acc = o_ref[...] + _dot(a_ref[...], w_ref[...])
    o_ref[...] = jnp.where(k == nk - 1, x_ref[...] + gate_ref[...] * acc, acc)


def _mm_resid(a, w, widx, x, mod, layer, gate_chunk, seq):
    n, kdim = a.shape
    d = w.shape[2]
    bm = _blk(seq, 1024, SUBLANE)
    bk = _blk(kdim, 4096 if kdim <= 4096 else 2048, LANE)
    nk = kdim // bk
    bn = _blk(d, 512 if nk == 1 else 1024, LANE)
    tpb = seq // bm
    return pl.pallas_call(
        functools.partial(_mm_resid_kernel, nk=nk),
        grid=(n // bm, d // bn, nk),
        in_specs=[
            pl.BlockSpec((bm, bk), lambda i, j, k: (i, k)),
            pl.BlockSpec((None, bk, bn), lambda i, j, k: (widx, k, j)),
            pl.BlockSpec((bm, bn), lambda i, j, k: (i, j)),
            _mod_spec(layer, gate_chunk, tpb, bn, 1),
        ],
        out_specs=pl.BlockSpec((bm, bn), lambda i, j, k: (i, j)),
        out_shape=jax.ShapeDtypeStruct((n, d), F32),
        compiler_params=_cparams(("parallel", "parallel", "arbitrary")),
        name="mm_resid",
    )(a, w, x, mod)


def _mlp_up_kernel(a_ref, w_ref, o_ref):
    o_ref[...] = jnp.square(jnp.maximum(_dot(a_ref[...], w_ref[...]), 0.0)).astype(o_ref.dtype)


def _mlp_up(a, w, widx):
    n, kdim = a.shape
    f = w.shape[2]
    bm = _blk(n, 1024, SUBLANE)
    bn = _blk(f, 1024, LANE)
    return pl.pallas_call(
        _mlp_up_kernel,
        grid=(n // bm, f // bn),
        in_specs=[pl.BlockSpec((bm, kdim), lambda i, j: (i, 0)),
                  pl.BlockSpec((None, kdim, bn), lambda i, j: (widx, 0, j))],
        out_specs=pl.BlockSpec((bm, bn), lambda i, j: (i, j)),
        out_shape=jax.ShapeDtypeStruct((n, f), BF16),
        compiler_params=_cparams(("parallel", "parallel")),
        name="mlp_up",
    )(a, w)


def _rkv_kernel(a_ref, w_ref, o_ref):
    o_ref[...] = _dot(a_ref[...], w_ref[...])


def _rkv(mixes, w_rkv, widx):
    _, n, d = mixes.shape
    bm = _blk(n, 1024, SUBLANE)
    bn = _blk(d, 1024, LANE)
    return pl.pallas_call(
        _rkv_kernel,
        grid=(3, n // bm, d // bn),
        in_specs=[pl.BlockSpec((None, bm, d), lambda s, i, j: (s, i, 0)),
                  pl.BlockSpec((None, None, d, bn), lambda s, i, j: (widx, s, 0, j))],
        out_specs=pl.BlockSpec((None, bm, bn), lambda s, i, j: (s, i, j)),
        out_shape=jax.ShapeDtypeStruct((3, n, d), F32),
        compiler_params=_cparams(("parallel", "parallel", "parallel")),
        name="rkv_proj",
    )(mixes, w_rkv)


def _prep_kernel(*refs, widths, use_v):
    if use_v:
        (hid_ref, wu_ref, k_ref, v_ref, vf_ref, w0_ref, a0_ref, kk_ref, ka_ref, v0_ref,
         lw_ref, a_ref, g_ref, kkn_ref, km_ref, vm_ref) = refs
    else:
        (hid_ref, wu_ref, k_ref, w0_ref, a0_ref, kk_ref, ka_ref,
         lw_ref, a_ref, g_ref, kkn_ref, km_ref) = refs
    wl, al, gl, vl = widths
    o = 0
    x = w0_ref[...] + _dot(hid_ref[:, o:o + wl], wu_ref[o:o + wl, :])
    o += wl
    sp = jnp.maximum(-x, 0.0) + jnp.log(1.0 + jnp.exp(-jnp.abs(x)))
    lw_ref[...] = -jnp.exp(-sp - 0.5)
    a = jax.nn.sigmoid(a0_ref[...] + _dot(hid_ref[:, o:o + al], wu_ref[o:o + al, :]))
    o += al
    a_ref[...] = a
    g_ref[...] = _dot(hid_ref[:, o:o + gl], wu_ref[o:o + gl, :])
    o += gl
    k = k_ref[...]
    kk = k * kk_ref[...]
    ss = _segsum(kk * kk, _head_ones(LANE))
    kkn_ref[...] = kk * lax.rsqrt(jnp.maximum(ss, 1e-24))
    km_ref[...] = k * (1.0 + (a - 1.0) * ka_ref[...])
    if use_v:
        v = v_ref[...]
        vg = jax.nn.sigmoid(v0_ref[...] + _dot(hid_ref[:, o:o + vl], wu_ref[o:o + vl, :]))
        vm_ref[...] = v + (vf_ref[...] - v) * vg


def _prep(hid, wu, rkv, v_first, w0, a0, k_k, k_a, v0, widths, use_v):
    n, hl = hid.shape
    d = wu.shape[1]
    bm = _blk(n, 512, SUBLANE)
    bn = _blk(d, 1024, LANE)
    row = lambda p: p.reshape(1, d)
    blk = pl.BlockSpec((bm, bn), lambda i, j: (i, j))
    prow = pl.BlockSpec((1, bn), lambda i, j: (0, j))
    ins = [hid, wu, rkv]
    specs = [pl.BlockSpec((bm, hl), lambda i, j: (i, 0)),
             pl.BlockSpec((hl, bn), lambda i, j: (0, j)),
             pl.BlockSpec((None, bm, bn), lambda i, j: (1, i, j))]
    if use_v:
        vf_arr, vf_lead = v_first
        ins += [rkv, vf_arr]
        specs += [pl.BlockSpec((None, bm, bn), lambda i, j: (2, i, j)),
                  pl.BlockSpec((None, bm, bn), lambda i, j: (vf_lead, i, j))]
    ins += [row(w0), row(a0), row(k_k), row(k_a)]
    specs += [prow] * 4
    n_out = 5
    if use_v:
        ins.append(row(v0))
        specs.append(prow)
        n_out = 6
    return pl.pallas_call(
        functools.partial(_prep_kernel, widths=widths, use_v=use_v),
        grid=(n // bm, d // bn),
        in_specs=specs,
        out_specs=[blk] * n_out,
        out_shape=[jax.ShapeDtypeStruct((n, d), F32)] * n_out,
        compiler_params=_cparams(("parallel", "parallel")),
        name="rwkv_prep",
    )(*ins)


def _wkv_chunk(s_prev, ins, consts):
    tri, head_masks, strict_w, incl_w, blk_w, same, eye = consts
    ng = range(len(ins))

    def stack(x):
        return jnp.concatenate([jnp.where(m, x, 0.0) for m in head_masks], axis=0).astype(BF16)

    def bdot(x, y):
        return _dot(x, y).astype(BF16)

    ar, kb_s, v_s, vk_l, vk_r, gl = [], [], [], [], [], []
    for r, lw, k, v, kk, a in ins:
        h1, h2, h3 = _split3(lw)
        cum = _dot(tri, h1) + _dot(tri, h2) + _dot(tri, h3)
        cl = cum[CHUNK - 1:CHUNK]
        gi = jnp.exp(-cum)
        b = kk * a
        dl = jnp.exp(cl - cum)
        ar.append(jnp.concatenate([-(kk * jnp.exp(cum - lw)), r * jnp.exp(cum)], axis=0).astype(BF16))
        kb_s.append(jnp.concatenate([stack(k * gi), stack(b * gi)], axis=0))
        v_s.append(stack(v))
        vk_l.append(v.astype(BF16))
        vk_r.append(jnp.concatenate([k * dl, b * dl], axis=0).astype(BF16))
        gl.append(jnp.exp(cl))

    sc = [_dot_nt(ar[g], kb_s[g]) for g in ng]
    a_ak = [jnp.where(strict_w, sc[g][0:CHUNK, 0:GROUP], 0.0).astype(BF16) for g in ng]
    a_rk = [jnp.where(incl_w, sc[g][CHUNK:2 * CHUNK, 0:GROUP], 0.0).astype(BF16) for g in ng]
    a_rb = [jnp.where(incl_w, sc[g][CHUNK:2 * CHUNK, GROUP:2 * GROUP], 0.0).astype(BF16) for g in ng]
    n_w = [jnp.where(strict_w, sc[g][0:CHUNK, GROUP:2 * GROUP], 0.0) for g in ng]
    dm = [stack(jnp.where(blk_w, n_w[g], 0.0)) for g in ng]
    e = [stack(jnp.where(blk_w, 0.0, n_w[g])) for g in ng]

    ss = [_dot_nt(ar[g], s_prev[g].astype(BF16)) for g in ng]
    pm = [stack(ss[g][0:CHUNK] + _dot(a_ak[g], v_s[g])) for g in ng]
    y0 = [ss[g][CHUNK:2 * CHUNK] + _dot(a_rk[g], v_s[g]) for g in ng]

    p = dm
    td = [eye + dm[g] for g in ng]
    for _ in range(3):
        p = [bdot(p[g], p[g]) for g in ng]
        td = [(td[g].astype(F32) + _dot(td[g], p[g])).astype(BF16) for g in ng]
    m1 = [bdot(td[g], e[g]) for g in ng]
    u = [bdot(td[g], pm[g]) for g in ng]
    m2 = [bdot(m1[g], m1[g]) for g in ng]
    w = [(u[g].astype(F32) + _dot(m2[g], u[g])).astype(BF16) for g in ng]
    sa = [(w[g].astype(F32) + _dot(m1[g], w[g])).astype(BF16) for g in ng]

    y, s_next = [], []
    for g in ng:
        y.append(y0[g] + _dot(a_rb[g], sa[g]))
    for g in ng:
        sa_n = sa[g][0:CHUNK]
        for hh in range(1, GROUP_HEADS):
            sa_n = sa_n + sa[g][hh * CHUNK:(hh + 1) * CHUNK]
        lhs = jnp.concatenate([vk_l[g], sa_n], axis=0)
        upd = _dot_tn(lhs, vk_r[g])
        s_next.append(s_prev[g] * gl[g] + jnp.where(same, upd, 0.0))
    return y, s_next


def _wkv_kernel(r_ref, lw_ref, k_ref, v_ref, kk_ref, a_ref, y_ref, s_ref, *, groups):
    c = pl.program_id(2)

    @pl.when(c == 0)
    def _():
        s_ref[...] = jnp.zeros_like(s_ref)

    ri = lax.broadcasted_iota(jnp.int32, (GROUP, GROUP), 0)
    ci = lax.broadcasted_iota(jnp.int32, (GROUP, GROUP), 1)
    same = (ri // CHUNK) == (ci // CHUNK)
    eye = jnp.where(ri == ci, 1.0, 0.0).astype(BF16)
    wt = lax.broadcasted_iota(jnp.int32, (CHUNK, GROUP), 0)
    ws = lax.broadcasted_iota(jnp.int32, (CHUNK, GROUP), 1) % CHUNK
    strict_w = wt > ws
    incl_w = wt >= ws
    blk_w = (wt // SUB) == (ws // SUB)
    ti = lax.broadcasted_iota(jnp.int32, (CHUNK, CHUNK), 0)
    tj = lax.broadcasted_iota(jnp.int32, (CHUNK, CHUNK), 1)
    tri = jnp.where(ti >= tj, 1.0, 0.0).astype(BF16)
    lane_head = lax.broadcasted_iota(jnp.int32, (CHUNK, GROUP), 1) // HEAD
    head_masks = [lane_head == hh for hh in range(GROUP_HEADS)]
    consts = (tri, head_masks, strict_w, incl_w, blk_w, same, eye)

    cols = [slice(g * GROUP, (g + 1) * GROUP) for g in range(groups)]
    ins = [tuple(ref[:, sl] for ref in (r_ref, lw_ref, k_ref, v_ref, kk_ref, a_ref)) for sl in cols]
    y, s_next = _wkv_chunk([s_ref[g] for g in range(groups)], ins, consts)
    for g, sl in enumerate(cols):
        y_ref[:, sl] = y[g]
        s_ref[g] = s_next[g]


def _wkv(r_src, lw, k, v_src, kk, a, bsz, seq):
    n, d = lw.shape
    groups = _blk(d, WKV_GROUPS * GROUP, GROUP) // GROUP
    gw = groups * GROUP
    nc = seq // CHUNK

    def spec(src):
        arr, lead = src
        if lead is None:
            return arr, pl.BlockSpec((CHUNK, gw), lambda bb, j, c: (bb * nc + c, j))
        return arr, pl.BlockSpec((None, CHUNK, gw), lambda bb, j, c: (lead, bb * nc + c, j))

    ins, specs = zip(*[spec(s) for s in (r_src, (lw, None), (k, None), v_src, (kk, None), (a, None))])
    return pl.pallas_call(
        functools.partial(_wkv_kernel, groups=groups),
        grid=(bsz, d // gw, nc),
        in_specs=list(specs),
        out_specs=pl.BlockSpec((CHUNK, gw), lambda bb, j, c: (bb * nc + c, j)),
        out_shape=jax.ShapeDtypeStruct((n, d), F32),
        scratch_shapes=[pltpu.VMEM((groups, GROUP, GROUP), F32)],
        compiler_params=_cparams(("parallel", "parallel", "arbitrary")),
        name="wkv7",
    )(*ins)


def _post_kernel(y_ref, r_ref, k_ref, v_ref, g_ref, lg_ref, lb_ref, rk_ref, o_ref):
    ones = _head_ones(LANE)
    y = y_ref[...]
    inv = 1.0 / HEAD
    dlt = y - _segsum(y, ones) * inv
    var = _segsum(dlt * dlt, ones) * inv
    yn = dlt * lax.rsqrt(var + GN_EPS) * lg_ref[...] + lb_ref[...]
    bonus = _segsum(r_ref[...] * k_ref[...] * rk_ref[...], ones) * v_ref[...]
    o_ref[...] = ((yn + bonus) * g_ref[...]).astype(o_ref.dtype)


def _post(y, rkv, km, v_src, g, lnx_g, lnx_b, r_k):
    n, d = y.shape
    bm = _blk(n, 512, SUBLANE)
    bn = _blk(d, 1024, LANE)
    blk = pl.BlockSpec((bm, bn), lambda i, j: (i, j))
    prow = pl.BlockSpec((1, bn), lambda i, j: (0, j))
    v_arr, v_lead = v_src
    v_spec = blk if v_lead is None else pl.BlockSpec((None, bm, bn), lambda i, j: (v_lead, i, j))
    return pl.pallas_call(
        _post_kernel,
        grid=(n // bm, d // bn),
        in_specs=[blk, pl.BlockSpec((None, bm, bn), lambda i, j: (0, i, j)), blk, v_spec, blk,
                  prow, prow, prow],
        out_specs=blk,
        out_shape=jax.ShapeDtypeStruct((n, d), BF16),
        compiler_params=_cparams(("parallel", "parallel")),
        name="rwkv_post",
    )(y, rkv, km, v_arr, g, lnx_g.reshape(1, d), lnx_b.reshape(1, d), r_k.reshape(1, d))


def _pad_to(w, axis, mult):
    size = w.shape[axis]
    target = -(-size // mult) * mult
    if target == size:
        return w
    pad = [(0, 0)] * w.ndim
    pad[axis] = (0, target - size)
    return jnp.pad(w, pad)


def kernel(x, c, ada_w, ada_b, ada_emb, ln1_g, ln2_g, mlp_w1, mlp_w2, conv_w_in, conv_w, conv_w_out, rwkv_mu, rwkv_w_rkv, rwkv_w0, rwkv_w1, rwkv_w2, rwkv_a0, rwkv_a1, rwkv_a2, rwkv_g1, rwkv_g2, rwkv_k_k, rwkv_k_a, rwkv_r_k, rwkv_lnx_g, rwkv_lnx_b, rwkv_w_o, rwkv_v0, rwkv_v1, rwkv_v2, final_g):
    bsz, seq, d = x.shape
    depth = ada_emb.shape[0]
    n = bsz * seq
    assert d % GROUP == 0 and seq % CHUNK == 0

    mod = _ada(c, ada_w, ada_b, ada_emb)
    w1_b, w2_b = mlp_w1.astype(BF16), mlp_w2.astype(BF16)
    cin_b, cout_b = conv_w_in.astype(BF16), conv_w_out.astype(BF16)
    rkv_b, wo_b = rwkv_w_rkv.astype(BF16), rwkv_w_o.astype(BF16)
    xs = x.reshape(n, d)
    v_first = None
    for i in range(depth):
        j = i // 2
        if i % 2 == 0:
            h = _norm_mod(xs, ln1_g[i], mod, i, 1, 0, seq)
            gated = _conv_in(h, cin_b, conv_w, j, seq)
            xs = _mm_resid(gated, cout_b, j, xs, mod, i, 2, seq)
        else:
            use_v = v_first is not None
            downs = [rwkv_w1[j], rwkv_a1[j], rwkv_g1[j]]
            ups = [rwkv_w2[j], rwkv_a2[j], rwkv_g2[j]]
            if use_v:
                downs.append(rwkv_v1[j - 1])
                ups.append(rwkv_v2[j - 1])
            downs = [_pad_to(w, 1, LANE) for w in downs]
            ups = [_pad_to(w, 0, LANE) for w in ups]
            widths = tuple(w.shape[1] for w in downs) + ((0,) if not use_v else ())
            wd = jnp.concatenate(downs, axis=1).astype(BF16)
            wu = jnp.concatenate(ups, axis=0).astype(BF16)
            mixes, hid = _mix(xs, ln1_g[i], mod, i, rwkv_mu[j], wd, widths, use_v, seq)
            rkv = _rkv(mixes, rkv_b, j)
            outs = _prep(hid, wu, rkv, v_first, rwkv_w0[j], rwkv_a0[j], rwkv_k_k[j],
                         rwkv_k_a[j], rwkv_v0[j - 1] if use_v else None, widths, use_v)
            if use_v:
                lw, a, g, kkn, km, vm = outs
                v_src = (vm, None)
            else:
                lw, a, g, kkn, km = outs
                v_src = (rkv, 2)
                v_first = (rkv, 2)
            y = _wkv((rkv, 0), lw, km, v_src, kkn, a, bsz, seq)
            gated = _post(y, rkv, km, v_src, g, rwkv_lnx_g[j], rwkv_lnx_b[j], rwkv_r_k[j])
            xs = _mm_resid(gated, wo_b, j, xs, mod, i, 2, seq)
        h = _norm_mod(xs, ln2_g[i], mod, i, 4, 3, seq)
        hidden = _mlp_up(h, w1_b, i)
        xs = _mm_resid(hidden, w2_b, i, xs, mod, i, 5, seq)
    return _final_norm(xs, final_g).reshape(bsz, seq, d)
```

```python
import functools

import jax
import jax.numpy as jnp
from jax import lax
from jax.experimental import pallas as pl
from jax.experimental.pallas import tpu as pltpu

F32 = jnp.float32
BF16 = jnp.bfloat16

HEAD = 64
CHUNK = 64
SUB = 16
GROUP_HEADS = 4
GROUP = GROUP_HEADS * HEAD
WKV_GROUPS = 16
LANE = 128
SUBLANE = 8
ADA_CHUNKS = 6
CONV_WIDTH = 3
RMS_EPS = 1e-5
GN_EPS = 64e-5
VMEM_LIMIT = 56 * 1024 * 1024


def _blk(n, pref, align):
    best = None
    d = align
    while d <= min(n, pref):
        if n % d == 0:
            best = d
        d += align
    return best if best is not None else n


def _cparams(sem):
    return pltpu.CompilerParams(dimension_semantics=sem, vmem_limit_bytes=VMEM_LIMIT)


def _dot(a, b):
    return jnp.dot(a, b, preferred_element_type=F32)


def _dot_nt(a, b):
    return lax.dot_general(a, b, (((1,), (1,)), ((), ())), preferred_element_type=F32)


def _dot_tn(a, b):
    return lax.dot_general(a, b, (((0,), (0,)), ((), ())), preferred_element_type=F32)


def _split3(x):
    h1 = x.astype(BF16)
    r1 = x - h1.astype(F32)
    h2 = r1.astype(BF16)
    h3 = (r1 - h2.astype(F32)).astype(BF16)
    return h1, h2, h3


def _head_ones(n):
    r = lax.broadcasted_iota(jnp.int32, (n, n), 0) // HEAD
    c = lax.broadcasted_iota(jnp.int32, (n, n), 1) // HEAD
    return jnp.where(r == c, 1.0, 0.0).astype(BF16)


def _segsum(x, ones):
    outs = []
    for s in range(x.shape[1] // LANE):
        h1, h2, h3 = _split3(x[:, s * LANE:(s + 1) * LANE])
        outs.append(_dot(h1, ones) + _dot(h2, ones) + _dot(h3, ones))
    return outs[0] if len(outs) == 1 else jnp.concatenate(outs, axis=1)


def _ada_kernel(c_ref, w_ref, b_ref, emb_ref, o_ref, *, depth):
    s = jax.nn.silu(c_ref[...])
    acc = jnp.dot(s, w_ref[...], preferred_element_type=F32,
                  precision=lax.Precision.HIGHEST) + b_ref[...]
    for i in range(depth):
        o_ref[i] = acc + emb_ref[i]


def _ada(c, ada_w, ada_b, ada_emb):
    bsz, d = c.shape
    depth = ada_emb.shape[0]
    n = ada_w.shape[1]
    rows = -(-bsz // SUBLANE) * SUBLANE
    cp = jnp.pad(c, ((0, rows - bsz), (0, 0)))
    bn = _blk(n, 512, LANE)
    out = pl.pallas_call(
        functools.partial(_ada_kernel, depth=depth),
        grid=(n // bn,),
        in_specs=[
            pl.BlockSpec((rows, d), lambda j: (0, 0)),
            pl.BlockSpec((d, bn), lambda j: (0, j)),
            pl.BlockSpec((1, bn), lambda j: (0, j)),
            pl.BlockSpec((depth, 1, bn), lambda j: (0, 0, j)),
        ],
        out_specs=pl.BlockSpec((depth, rows, bn), lambda j: (0, 0, j)),
        out_shape=jax.ShapeDtypeStruct((depth, rows, n), F32),
        compiler_params=_cparams(("parallel",)),
        name="ada_mod",
    )(cp, ada_w, ada_b.reshape(1, n), ada_emb.reshape(depth, 1, n))
    mod = out[:, :bsz].reshape(depth, bsz, ADA_CHUNKS, d)
    return jnp.transpose(mod, (0, 2, 1, 3)).reshape(depth, ADA_CHUNKS, bsz, 1, d)


def _mod_spec(layer, chunk, tpb, bn, col_axis):
    if col_axis is None:
        return pl.BlockSpec((None, None, None, 1, bn), lambda i: (layer, chunk, i // tpb, 0, 0))
    return pl.BlockSpec((None, None, None, 1, bn),
                        lambda i, j, *_: (layer, chunk, i // tpb, 0, j))


def _rms(x, g):
    ms = jnp.mean(x * x, axis=-1, keepdims=True)
    return (x * lax.rsqrt(ms + RMS_EPS)) * g


def _norm_mod_kernel(x_ref, g_ref, sc_ref, sh_ref, o_ref):
    y = _rms(x_ref[...], g_ref[...])
    o_ref[...] = (y * (1.0 + sc_ref[...]) + sh_ref[...]).astype(o_ref.dtype)


def _norm_mod(x, g, mod, layer, sc_chunk, sh_chunk, seq):
    n, d = x.shape
    bm = _blk(seq, 512, SUBLANE)
    tpb = seq // bm
    return pl.pallas_call(
        _norm_mod_kernel,
        grid=(n // bm,),
        in_specs=[
            pl.BlockSpec((bm, d), lambda i: (i, 0)),
            pl.BlockSpec((1, d), lambda i: (0, 0)),
            _mod_spec(layer, sc_chunk, tpb, d, None),
            _mod_spec(layer, sh_chunk, tpb, d, None),
        ],
        out_specs=pl.BlockSpec((bm, d), lambda i: (i, 0)),
        out_shape=jax.ShapeDtypeStruct((n, d), BF16),
        compiler_params=_cparams(("parallel",)),
        name="norm_mod",
    )(x, g.reshape(1, d), mod, mod)


def _final_norm_kernel(x_ref, g_ref, o_ref):
    o_ref[...] = _rms(x_ref[...], g_ref[...])


def _final_norm(x, g):
    n, d = x.shape
    bm = _blk(n, 512, SUBLANE)
    return pl.pallas_call(
        _final_norm_kernel,
        grid=(n // bm,),
        in_specs=[pl.BlockSpec((bm, d), lambda i: (i, 0)),
                  pl.BlockSpec((1, d), lambda i: (0, 0))],
        out_specs=pl.BlockSpec((bm, d), lambda i: (i, 0)),
        out_shape=jax.ShapeDtypeStruct((n, d), F32),
        compiler_params=_cparams(("parallel",)),
        name="final_norm",
    )(x, g.reshape(1, d))


def _mix_kernel(x_ref, xp_ref, g_ref, sc_ref, sh_ref, mu_ref, wd_ref, mix_ref, hid_ref,
                *, tpb, widths, use_v):
    i = pl.program_id(0)

    def nm(x):
        return _rms(x, g_ref[...]) * (1.0 + sc_ref[...]) + sh_ref[...]

    h = nm(x_ref[...])
    hp = nm(xp_ref[...])[SUBLANE - 1:SUBLANE]
    hp = jnp.where(i % tpb == 0, 0.0, hp)
    rows = lax.broadcasted_iota(jnp.int32, h.shape, 0)
    xx = jnp.where(rows == 0, hp, pltpu.roll(h, 1, 0)) - h

    def mix(j):
        return (h + xx * mu_ref[j:j + 1, :]).astype(BF16)

    mix_ref[0] = mix(0)
    mix_ref[1] = mix(1)
    xv = mix(2)
    mix_ref[2] = xv
    wl, al, gl, vl = widths
    o = 0
    hid_ref[:, o:o + wl] = jnp.tanh(_dot(mix(3), wd_ref[:, o:o + wl])).astype(BF16)
    o += wl
    hid_ref[:, o:o + al] = _dot(mix(4), wd_ref[:, o:o + al]).astype(BF16)
    o += al
    hid_ref[:, o:o + gl] = jax.nn.sigmoid(_dot(mix(5), wd_ref[:, o:o + gl])).astype(BF16)
    o += gl
    if use_v:
        hid_ref[:, o:o + vl] = _dot(xv, wd_ref[:, o:o + vl]).astype(BF16)


def _mix(x, g, mod, layer, mu, wd, widths, use_v, seq):
    n, d = x.shape
    hl = wd.shape[1]
    bm = _blk(seq, 128, SUBLANE)
    tpb = seq // bm
    per8 = bm // SUBLANE
    return pl.pallas_call(
        functools.partial(_mix_kernel, tpb=tpb, widths=widths, use_v=use_v),
        grid=(n // bm,),
        in_specs=[
            pl.BlockSpec((bm, d), lambda i: (i, 0)),
            pl.BlockSpec((SUBLANE, d), lambda i: (jnp.maximum(i * per8 - 1, 0), 0)),
            pl.BlockSpec((1, d), lambda i: (0, 0)),
            _mod_spec(layer, 1, tpb, d, None),
            _mod_spec(layer, 0, tpb, d, None),
            pl.BlockSpec((6, d), lambda i: (0, 0)),
            pl.BlockSpec((d, hl), lambda i: (0, 0)),
        ],
        out_specs=[pl.BlockSpec((3, bm, d), lambda i: (0, i, 0)),
                   pl.BlockSpec((bm, hl), lambda i: (i, 0))],
        out_shape=[jax.ShapeDtypeStruct((3, n, d), BF16),
                   jax.ShapeDtypeStruct((n, hl), BF16)],
        compiler_params=_cparams(("parallel",)),
        name="rwkv_mix",
    )(x, x, g.reshape(1, d), mod, mod, mu, wd)


def _conv_in_kernel(h_ref, wb_ref, wc_ref, wu_ref, cw_ref, o_ref, carry_ref, *, tpb):
    m = pl.program_id(1)
    h = h_ref[...]
    bg = _dot(h, wb_ref[...])
    z = _dot(h, wc_ref[...]) * _dot(h, wu_ref[...])
    bm = z.shape[0]
    prev = jnp.where(m % tpb == 0, 0.0, carry_ref[...])
    carry_ref[...] = z[bm - SUBLANE:bm]
    w0 = cw_ref[0:1, :]
    w1 = cw_ref[1:2, :]
    w2 = cw_ref[2:3, :]
    y = w0 * pltpu.roll(z, 2, 0) + w1 * pltpu.roll(z, 1, 0) + w2 * z
    o_ref[...] = (bg * y).astype(o_ref.dtype)
    zt = z[0:SUBLANE]
    r8 = lax.broadcasted_iota(jnp.int32, zt.shape, 0)
    z1 = jnp.where(r8 < 1, pltpu.roll(prev, 1, 0), pltpu.roll(zt, 1, 0))
    z2 = jnp.where(r8 < 2, pltpu.roll(prev, 2, 0), pltpu.roll(zt, 2, 0))
    yt = w0 * z2 + w1 * z1 + w2 * zt
    o_ref[0:SUBLANE, :] = (bg[0:SUBLANE] * yt).astype(o_ref.dtype)


def _conv_in(h, w_in, conv_w, layer, seq):
    n, d = h.shape
    bm = _blk(seq, 1024, SUBLANE)
    bn = _blk(d, 256, LANE)
    tpb = seq // bm
    nj = d // bn
    return pl.pallas_call(
        functools.partial(_conv_in_kernel, tpb=tpb),
        grid=(nj, n // bm),
        in_specs=[
            pl.BlockSpec((bm, d), lambda j, i: (i, 0)),
            pl.BlockSpec((None, d, bn), lambda j, i: (layer, 0, j)),
            pl.BlockSpec((None, d, bn), lambda j, i: (layer, 0, j + nj)),
            pl.BlockSpec((None, d, bn), lambda j, i: (layer, 0, j + 2 * nj)),
            pl.BlockSpec((None, CONV_WIDTH, bn), lambda j, i: (layer, 0, j)),
        ],
        out_specs=pl.BlockSpec((bm, bn), lambda j, i: (i, j)),
        out_shape=jax.ShapeDtypeStruct((n, d), BF16),
        scratch_shapes=[pltpu.VMEM((SUBLANE, bn), F32)],
        compiler_params=_cparams(("arbitrary", "arbitrary")),
        name="conv_in",
    )(h, w_in, w_in, w_in, conv_w)


def _mm_resid_kernel(a_ref, w_ref, x_ref, gate_ref, o_ref, *, nk):
    if nk == 1:
        o_ref[...] = x_ref[...] + gate_ref[...] * _dot(a_ref[...], w_ref[...])
        return
    k = pl.program_id(2)

    @pl.when(k == 0)
    def _():
        o_ref[...] = jnp.zeros_like(o_ref)

    acc = o_ref[...] + _dot(a_ref[...], w_ref[...])
    o_ref[...] = jnp.where(k == nk - 1, x_ref[...] + gate_ref[...] * acc, acc)


def _mm_resid(a, w, widx, x, mod, layer, gate_chunk, seq):
    n, kdim = a.shape
    d = w.shape[2]
    bm = _blk(seq, 1024, SUBLANE)
    bk = _blk(kdim, 4096 if kdim <= 4096 else 2048, LANE)
    nk = kdim // bk
    bn = _blk(d, 512 if nk == 1 else 1024, LANE)
    tpb = seq // bm
    return pl.pallas_call(
        functools.partial(_mm_resid_kernel, nk=nk),
        grid=(n // bm, d // bn, nk),
        in_specs=[
            pl.BlockSpec((bm, bk), lambda i, j, k: (i, k)),
            pl.BlockSpec((None, bk, bn), lambda i, j, k: (widx, k, j)),
            pl.BlockSpec((bm, bn), lambda i, j, k: (i, j)),
            _mod_spec(layer, gate_chunk, tpb, bn, 1),
        ],
        out_specs=pl.BlockSpec((bm, bn), lambda i, j, k: (i, j)),
        out_shape=jax.ShapeDtypeStruct((n, d), F32),
        compiler_params=_cparams(("parallel", "parallel", "arbitrary")),
        name="mm_resid",
    )(a, w, x, mod)


def _mlp_up_kernel(a_ref, w_ref, o_ref):
    o_ref[...] = jnp.square(jnp.maximum(_dot(a_ref[...], w_ref[...]), 0.0)).astype(o_ref.dtype)


def _mlp_up(a, w, widx):
    n, kdim = a.shape
    f = w.shape[2]
    bm = _blk(n, 1024, SUBLANE)
    bn = _blk(f, 1024, LANE)
    return pl.pallas_call(
        _mlp_up_kernel,
        grid=(n // bm, f // bn),
        in_specs=[pl.BlockSpec((bm, kdim), lambda i, j: (i, 0)),
                  pl.BlockSpec((None, kdim, bn), lambda i, j: (widx, 0, j))],
        out_specs=pl.BlockSpec((bm, bn), lambda i, j: (i, j)),
        out_shape=jax.ShapeDtypeStruct((n, f), BF16),
        compiler_params=_cparams(("parallel", "parallel")),
        name="mlp_up",
    )(a, w)


def _rkv_kernel(a_ref, w_ref, o_ref):
    o_ref[...] = _dot(a_ref[...], w_ref[...])


def _rkv(mixes, w_rkv, widx):
    _, n, d = mixes.shape
    bm = _blk(n, 1024, SUBLANE)
    bn = _blk(d, 1024, LANE)
    return pl.pallas_call(
        _rkv_kernel,
        grid=(3, n // bm, d // bn),
        in_specs=[pl.BlockSpec((None, bm, d), lambda s, i, j: (s, i, 0)),
                  pl.BlockSpec((None, None, d, bn), lambda s, i, j: (widx, s, 0, j))],
        out_specs=pl.BlockSpec((None, bm, bn), lambda s, i, j: (s, i, j)),
        out_shape=jax.ShapeDtypeStruct((3, n, d), F32),
        compiler_params=_cparams(("parallel", "parallel", "parallel")),
        name="rkv_proj",
    )(mixes, w_rkv)


def _prep_kernel(*refs, widths, use_v):
    if use_v:
        (hid_ref, wu_ref, k_ref, v_ref, vf_ref, w0_ref, a0_ref, kk_ref, ka_ref, v0_ref,
         lw_ref, a_ref, g_ref, kkn_ref, km_ref, vm_ref) = refs
    else:
        (hid_ref, wu_ref, k_ref, w0_ref, a0_ref, kk_ref, ka_ref,
         lw_ref, a_ref, g_ref, kkn_ref, km_ref) = refs
    wl, al, gl, vl = widths
    o = 0
    x = w0_ref[...] + _dot(hid_ref[:, o:o + wl], wu_ref[o:o + wl, :])
    o += wl
    sp = jnp.maximum(-x, 0.0) + jnp.log(1.0 + jnp.exp(-jnp.abs(x)))
    lw_ref[...] = -jnp.exp(-sp - 0.5)
    a = jax.nn.sigmoid(a0_ref[...] + _dot(hid_ref[:, o:o + al], wu_ref[o:o + al, :]))
    o += al
    a_ref[...] = a
    g_ref[...] = _dot(hid_ref[:, o:o + gl], wu_ref[o:o + gl, :])
    o += gl
    k = k_ref[...]
    kk = k * kk_ref[...]
    ss = _segsum(kk * kk, _head_ones(LANE))
    kkn_ref[...] = kk * lax.rsqrt(jnp.maximum(ss, 1e-24))
    km_ref[...] = k * (1.0 + (a - 1.0) * ka_ref[...])
    if use_v:
        v = v_ref[...]
        vg = jax.nn.sigmoid(v0_ref[...] + _dot(hid_ref[:, o:o + vl], wu_ref[o:o + vl, :]))
        vm_ref[...] = v + (vf_ref[...] - v) * vg


def _prep(hid, wu, rkv, v_first, w0, a0, k_k, k_a, v0, widths, use_v):
    n, hl = hid.shape
    d = wu.shape[1]
    bm = _blk(n, 256, SUBLANE)
    bn = _blk(d, 2048, LANE)
    row = lambda p: p.reshape(1, d)
    blk = pl.BlockSpec((bm, bn), lambda i, j: (i, j))
    prow = pl.BlockSpec((1, bn), lambda i, j: (0, j))
    ins = [hid, wu, rkv]
    specs = [pl.BlockSpec((bm, hl), lambda i, j: (i, 0)),
             pl.BlockSpec((hl, bn), lambda i, j: (0, j)),
             pl.BlockSpec((None, bm, bn), lambda i, j: (1, i, j))]
    if use_v:
        vf_arr, vf_lead = v_first
        ins += [rkv, vf_arr]
        specs += [pl.BlockSpec((None, bm, bn), lambda i, j: (2, i, j)),
                  pl.BlockSpec((None, bm, bn), lambda i, j: (vf_lead, i, j))]
    ins += [row(w0), row(a0), row(k_k), row(k_a)]
    specs += [prow] * 4
    n_out = 5
    if use_v:
        ins.append(row(v0))
        specs.append(prow)
        n_out = 6
    return pl.pallas_call(
        functools.partial(_prep_kernel, widths=widths, use_v=use_v),
        grid=(n // bm, d // bn),
        in_specs=specs,
        out_specs=[blk] * n_out,
        out_shape=[jax.ShapeDtypeStruct((n, d), F32)] * n_out,
        compiler_params=_cparams(("parallel", "parallel")),
        name="rwkv_prep",
    )(*ins)


def _wkv_chunk(s_prev, ins, consts):
    tri, head_masks, strict_w, incl_w, blk_w, same, eye = consts
    ng = range(len(ins))

    def stack(x):
        return jnp.concatenate([jnp.where(m, x, 0.0) for m in head_masks], axis=0).astype(BF16)

    def bdot(x, y):
        return _dot(x, y).astype(BF16)

    ar, kb_s, v_s, vk_l, vk_r, gl = [], [], [], [], [], []
    for r, lw, k, v, kk, a in ins:
        h1, h2, h3 = _split3(lw)
        cum = _dot(tri, h1) + _dot(tri, h2) + _dot(tri, h3)
        cl = cum[CHUNK - 1:CHUNK]
        gi = jnp.exp(-cum)
        b = kk * a
        dl = jnp.exp(cl - cum)
        ar.append(jnp.concatenate([-(kk * jnp.exp(cum - lw)), r * jnp.exp(cum)], axis=0).astype(BF16))
        kb_s.append(jnp.concatenate([stack(k * gi), stack(b * gi)], axis=0))
        v_s.append(stack(v))
        vk_l.append(v.astype(BF16))
        vk_r.append(jnp.concatenate([k * dl, b * dl], axis=0).astype(BF16))
        gl.append(jnp.exp(cl))

    sc = [_dot_nt(ar[g], kb_s[g]) for g in ng]
    a_ak = [jnp.where(strict_w, sc[g][0:CHUNK, 0:GROUP], 0.0).astype(BF16) for g in ng]
    a_rk = [jnp.where(incl_w, sc[g][CHUNK:2 * CHUNK, 0:GROUP], 0.0).astype(BF16) for g in ng]
    a_rb = [jnp.where(incl_w, sc[g][CHUNK:2 * CHUNK, GROUP:2 * GROUP], 0.0).astype(BF16) for g in ng]
    n_w = [jnp.where(strict_w, sc[g][0:CHUNK, GROUP:2 * GROUP], 0.0) for g in ng]
    dm = [stack(jnp.where(blk_w, n_w[g], 0.0)) for g in ng]
    e = [stack(jnp.where(blk_w, 0.0, n_w[g])) for g in ng]

    ss = [_dot_nt(ar[g], s_prev[g].astype(BF16)) for g in ng]
    pm = [stack(ss[g][0:CHUNK] + _dot(a_ak[g], v_s[g])) for g in ng]
    y0 = [ss[g][CHUNK:2 * CHUNK] + _dot(a_rk[g], v_s[g]) for g in ng]

    p = dm
    td = [eye + dm[g] for g in ng]
    for _ in range(3):
        p = [bdot(p[g], p[g]) for g in ng]
        td = [(td[g].astype(F32) + _dot(td[g], p[g])).astype(BF16) for g in ng]
    m1 = [bdot(td[g], e[g]) for g in ng]
    u = [bdot(td[g], pm[g]) for g in ng]
    m2 = [bdot(m1[g], m1[g]) for g in ng]
    w = [(u[g].astype(F32) + _dot(m2[g], u[g])).astype(BF16) for g in ng]
    sa = [(w[g].astype(F32) + _dot(m1[g], w[g])).astype(BF16) for g in ng]

    y, s_next = [], []
    for g in ng:
        y.append(y0[g] + _dot(a_rb[g], sa[g]))
    for g in ng:
        sa_n = sa[g][0:CHUNK]
        for hh in range(1, GROUP_HEADS):
            sa_n = sa_n + sa[g][hh * CHUNK:(hh + 1) * CHUNK]
        lhs = jnp.concatenate([vk_l[g], sa_n], axis=0)
        upd = _dot_tn(lhs, vk_r[g])
        s_next.append(s_prev[g] * gl[g] + jnp.where(same, upd, 0.0))
    return y, s_next


def _wkv_kernel(r_ref, lw_ref, k_ref, v_ref, kk_ref, a_ref, y_ref, s_ref, *, groups):
    c = pl.program_id(2)

    @pl.when(c == 0)
    def _():
        s_ref[...] = jnp.zeros_like(s_ref)

    ri = lax.broadcasted_iota(jnp.int32, (GROUP, GROUP), 0)
    ci = lax.broadcasted_iota(jnp.int32, (GROUP, GROUP), 1)
    same = (ri // CHUNK) == (ci // CHUNK)
    eye = jnp.where(ri == ci, 1.0, 0.0).astype(BF16)
    wt = lax.broadcasted_iota(jnp.int32, (CHUNK, GROUP), 0)
    ws = lax.broadcasted_iota(jnp.int32, (CHUNK, GROUP), 1) % CHUNK
    strict_w = wt > ws
    incl_w = wt >= ws
    blk_w = (wt // SUB) == (ws // SUB)
    ti = lax.broadcasted_iota(jnp.int32, (CHUNK, CHUNK), 0)
    tj = lax.broadcasted_iota(jnp.int32, (CHUNK, CHUNK), 1)
    tri = jnp.where(ti >= tj, 1.0, 0.0).astype(BF16)
    lane_head = lax.broadcasted_iota(jnp.int32, (CHUNK, GROUP), 1) // HEAD
    head_masks = [lane_head == hh for hh in range(GROUP_HEADS)]
    consts = (tri, head_masks, strict_w, incl_w, blk_w, same, eye)

    cols = [slice(g * GROUP, (g + 1) * GROUP) for g in range(groups)]
    ins = [tuple(ref[:, sl] for ref in (r_ref, lw_ref, k_ref, v_ref, kk_ref, a_ref)) for sl in cols]
    y, s_next = _wkv_chunk([s_ref[g] for g in range(groups)], ins, consts)
    for g, sl in enumerate(cols):
        y_ref[:, sl] = y[g]
        s_ref[g] = s_next[g]


def _wkv(r_src, lw, k, v_src, kk, a, bsz, seq):
    n, d = lw.shape
    groups = _blk(d, WKV_GROUPS * GROUP, GROUP) // GROUP
    gw = groups * GROUP
    nc = seq // CHUNK

    def spec(src):
        arr, lead = src
        if lead is None:
            return arr, pl.BlockSpec((CHUNK, gw), lambda bb, j, c: (bb * nc + c, j))
        return arr, pl.BlockSpec((None, CHUNK, gw), lambda bb, j, c: (lead, bb * nc + c, j))

    ins, specs = zip(*[spec(s) for s in (r_src, (lw, None), (k, None), v_src, (kk, None), (a, None))])
    return pl.pallas_call(
        functools.partial(_wkv_kernel, groups=groups),
        grid=(bsz, d // gw, nc),
        in_specs=list(specs),
        out_specs=pl.BlockSpec((CHUNK, gw), lambda bb, j, c: (bb * nc + c, j)),
        out_shape=jax.ShapeDtypeStruct((n, d), F32),
        scratch_shapes=[pltpu.VMEM((groups, GROUP, GROUP), F32)],
        compiler_params=_cparams(("parallel", "parallel", "arbitrary")),
        name="wkv7",
    )(*ins)


def _post_kernel(y_ref, r_ref, k_ref, v_ref, g_ref, lg_ref, lb_ref, rk_ref, o_ref):
    ones = _head_ones(LANE)
    y = y_ref[...]
    inv = 1.0 / HEAD
    dlt = y - _segsum(y, ones) * inv
    var = _segsum(dlt * dlt, ones) * inv
    yn = dlt * lax.rsqrt(var + GN_EPS) * lg_ref[...] + lb_ref[...]
    bonus = _segsum(r_ref[...] * k_ref[...] * rk_ref[...], ones) * v_ref[...]
    o_ref[...] = ((yn + bonus) * g_ref[...]).astype(o_ref.dtype)


def _post(y, rkv, km, v_src, g, lnx_g, lnx_b, r_k):
    n, d = y.shape
    bm = _blk(n, 256, SUBLANE)
    bn = _blk(d, 2048, LANE)
    blk = pl.BlockSpec((bm, bn), lambda i, j: (i, j))
    prow = pl.BlockSpec((1, bn), lambda i, j: (0, j))
    v_arr, v_lead = v_src
    v_spec = blk if v_lead is None else pl.BlockSpec((None, bm, bn), lambda i, j: (v_lead, i, j))
    return pl.pallas_call(
        _post_kernel,
        grid=(n // bm, d // bn),
        in_specs=[blk, pl.BlockSpec((None, bm, bn), lambda i, j: (0, i, j)), blk, v_spec, blk,
                  prow, prow, prow],
        out_specs=blk,
        out_shape=jax.ShapeDtypeStruct((n, d), BF16),
        compiler_params=_cparams(("parallel", "parallel")),
        name="rwkv_post",
    )(y, rkv, km, v_arr, g, lnx_g.reshape(1, d), lnx_b.reshape(1, d), r_k.reshape(1, d))


def _pad_to(w, axis, mult):
    size = w.shape[axis]
    target = -(-size // mult) * mult
    if target == size:
        return w
    pad = [(0, 0)] * w.ndim
    pad[axis] = (0, target - size)
    return jnp.pad(w, pad)


def kernel(x, c, ada_w, ada_b, ada_emb, ln1_g, ln2_g, mlp_w1, mlp_w2, conv_w_in, conv_w, conv_w_out, rwkv_mu, rwkv_w_rkv, rwkv_w0, rwkv_w1, rwkv_w2, rwkv_a0, rwkv_a1, rwkv_a2, rwkv_g1, rwkv_g2, rwkv_k_k, rwkv_k_a, rwkv_r_k, rwkv_lnx_g, rwkv_lnx_b, rwkv_w_o, rwkv_v0, rwkv_v1, rwkv_v2, final_g):
    bsz, seq, d = x.shape
    depth = ada_emb.shape[0]
    n = bsz * seq
    assert d % GROUP == 0 and seq % CHUNK == 0

    mod = _ada(c, ada_w, ada_b, ada_emb)
    w1_b, w2_b = mlp_w1.astype(BF16), mlp_w2.astype(BF16)
    cin_b, cout_b = conv_w_in.astype(BF16), conv_w_out.astype(BF16)
    rkv_b, wo_b = rwkv_w_rkv.astype(BF16), rwkv_w_o.astype(BF16)
    xs = x.reshape(n, d)
    v_first = None
    for i in range(depth):
        j = i // 2
        if i % 2 == 0:
            h = _norm_mod(xs, ln1_g[i], mod, i, 1, 0, seq)
            gated = _conv_in(h, cin_b, conv_w, j, seq)
            xs = _mm_resid(gated, cout_b, j, xs, mod, i, 2, seq)
        else:
            use_v = v_first is not None
            downs = [rwkv_w1[j], rwkv_a1[j], rwkv_g1[j]]
            ups = [rwkv_w2[j], rwkv_a2[j], rwkv_g2[j]]
            if use_v:
                downs.append(rwkv_v1[j - 1])
                ups.append(rwkv_v2[j - 1])
            downs = [_pad_to(w, 1, LANE) for w in downs]
            ups = [_pad_to(w, 0, LANE) for w in ups]
            widths = tuple(w.shape[1] for w in downs) + ((0,) if not use_v else ())
            wd = jnp.concatenate(downs, axis=1).astype(BF16)
            wu = jnp.concatenate(ups, axis=0).astype(BF16)
            mixes, hid = _mix(xs, ln1_g[i], mod, i, rwkv_mu[j], wd, widths, use_v, seq)
            rkv = _rkv(mixes, rkv_b, j)
            outs = _prep(hid, wu, rkv, v_first, rwkv_w0[j], rwkv_a0[j], rwkv_k_k[j],
                         rwkv_k_a[j], rwkv_v0[j - 1] if use_v else None, widths, use_v)
            if use_v:
                lw, a, g, kkn, km, vm = outs
                v_src = (vm, None)
            else:
                lw, a, g, kkn, km = outs
                v_src = (rkv, 2)
                v_first = (rkv, 2)
            y = _wkv((rkv, 0), lw, km, v_src, kkn, a, bsz, seq)
            gated = _post(y, rkv, km, v_src, g, rwkv_lnx_g[j], rwkv_lnx_b[j], rwkv_r_k[j])
            xs = _mm_resid(gated, wo_b, j, xs, mod, i, 2, seq)
        h = _norm_mod(xs, ln2_g[i], mod, i, 4, 3, seq)
        hidden = _mlp_up(h, w1_b, i)
        xs = _mm_resid(hidden, w2_b, i, xs, mod, i, 5, seq)
    return _final_norm(xs, final_g).reshape(bsz, seq, d)
```

```python
import functools

import jax
import jax.numpy as jnp
from jax import lax
from jax.experimental import pallas as pl
from jax.experimental.pallas import tpu as pltpu

F32 = jnp.float32
BF16 = jnp.bfloat16

HEAD = 64
CHUNK = 64
SUB = 16
GROUP_HEADS = 4
GROUP = GROUP_HEADS * HEAD
WKV_GROUPS = 16
LANE = 128
SUBLANE = 8
ADA_CHUNKS = 6
CONV_WIDTH = 3
RMS_EPS = 1e-5
GN_EPS = 64e-5
VMEM_LIMIT = 56 * 1024 * 1024


def _blk(n, pref, align):
    best = None
    d = align
    while d <= min(n, pref):
        if n % d == 0:
            best = d
        d += align
    return best if best is not None else n


def _cparams(sem):
    return pltpu.CompilerParams(dimension_semantics=sem, vmem_limit_bytes=VMEM_LIMIT)


def _dot(a, b):
    return jnp.dot(a, b, preferred_element_type=F32)


def _dot_nt(a, b):
    return lax.dot_general(a, b, (((1,), (1,)), ((), ())), preferred_element_type=F32)


def _dot_tn(a, b):
    return lax.dot_general(a, b, (((0,), (0,)), ((), ())), preferred_element_type=F32)


def _split3(x):
    h1 = x.astype(BF16)
    r1 = x - h1.astype(F32)
    h2 = r1.astype(BF16)
    h3 = (r1 - h2.astype(F32)).astype(BF16)
    return h1, h2, h3


def _head_ones(n):
    r = lax.broadcasted_iota(jnp.int32, (n, n), 0) // HEAD
    c = lax.broadcasted_iota(jnp.int32, (n, n), 1) // HEAD
    return jnp.where(r == c, 1.0, 0.0).astype(BF16)


def _segsum(x, ones):
    outs = []
    for s in range(x.shape[1] // LANE):
        h1, h2, h3 = _split3(x[:, s * LANE:(s + 1) * LANE])
        outs.append(_dot(h1, ones) + _dot(h2, ones) + _dot(h3, ones))
    return outs[0] if len(outs) == 1 else jnp.concatenate(outs, axis=1)


def _ada_kernel(c_ref, w_ref, b_ref, emb_ref, o_ref, *, depth):
    s = jax.nn.silu(c_ref[...])
    acc = jnp.dot(s, w_ref[...], preferred_element_type=F32,
                  precision=lax.Precision.HIGHEST) + b_ref[...]
    for i in range(depth):
        o_ref[i] = acc + emb_ref[i]


def _ada(c, ada_w, ada_b, ada_emb):
    bsz, d = c.shape
    depth = ada_emb.shape[0]
    n = ada_w.shape[1]
    rows = -(-bsz // SUBLANE) * SUBLANE
    cp = jnp.pad(c, ((0, rows - bsz), (0, 0)))
    bn = _blk(n, 512, LANE)
    out = pl.pallas_call(
        functools.partial(_ada_kernel, depth=depth),
        grid=(n // bn,),
        in_specs=[
            pl.BlockSpec((rows, d), lambda j: (0, 0)),
            pl.BlockSpec((d, bn), lambda j: (0, j)),
            pl.BlockSpec((1, bn), lambda j: (0, j)),
            pl.BlockSpec((depth, 1, bn), lambda j: (0, 0, j)),
        ],
        out_specs=pl.BlockSpec((depth, rows, bn), lambda j: (0, 0, j)),
        out_shape=jax.ShapeDtypeStruct((depth, rows, n), F32),
        compiler_params=_cparams(("parallel",)),
        name="ada_mod",
    )(cp, ada_w, ada_b.reshape(1, n), ada_emb.reshape(depth, 1, n))
    mod = out[:, :bsz].reshape(depth, bsz, ADA_CHUNKS, d)
    return jnp.transpose(mod, (0, 2, 1, 3)).reshape(depth, ADA_CHUNKS, bsz, 1, d)


def _mod_spec(layer, chunk, tpb, bn, col_axis):
    if col_axis is None:
        return pl.BlockSpec((None, None, None, 1, bn), lambda i: (layer, chunk, i // tpb, 0, 0))
    return pl.BlockSpec((None, None, None, 1, bn),
                        lambda i, j, *_: (layer, chunk, i // tpb, 0, j))


def _rms(x, g):
    ms = jnp.mean(x * x, axis=-1, keepdims=True)
    return (x * lax.rsqrt(ms + RMS_EPS)) * g


def _norm_mod_kernel(x_ref, g_ref, sc_ref, sh_ref, o_ref):
    y = _rms(x_ref[...], g_ref[...])
    o_ref[...] = (y * (1.0 + sc_ref[...]) + sh_ref[...]).astype(o_ref.dtype)


def _norm_mod(x, g, mod, layer, sc_chunk, sh_chunk, seq):
    n, d = x.shape
    bm = _blk(seq, 512, SUBLANE)
    tpb = seq // bm
    return pl.pallas_call(
        _norm_mod_kernel,
        grid=(n // bm,),
        in_specs=[
            pl.BlockSpec((bm, d), lambda i: (i, 0)),
            pl.BlockSpec((1, d), lambda i: (0, 0)),
            _mod_spec(layer, sc_chunk, tpb, d, None),
            _mod_spec(layer, sh_chunk, tpb, d, None),
        ],
        out_specs=pl.BlockSpec((bm, d), lambda i: (i, 0)),
        out_shape=jax.ShapeDtypeStruct((n, d), BF16),
        compiler_params=_cparams(("parallel",)),
        name="norm_mod",
    )(x, g.reshape(1, d), mod, mod)


def _final_norm_kernel(x_ref, g_ref, o_ref):
    o_ref[...] = _rms(x_ref[...], g_ref[...])


def _final_norm(x, g):
    n, d = x.shape
    bm = _blk(n, 512, SUBLANE)
    return pl.pallas_call(
        _final_norm_kernel,
        grid=(n // bm,),
        in_specs=[pl.BlockSpec((bm, d), lambda i: (i, 0)),
                  pl.BlockSpec((1, d), lambda i: (0, 0))],
        out_specs=pl.BlockSpec((bm, d), lambda i: (i, 0)),
        out_shape=jax.ShapeDtypeStruct((n, d), F32),
        compiler_params=_cparams(("parallel",)),
        name="final_norm",
    )(x, g.reshape(1, d))


def _mix_kernel(x_ref, xp_ref, g_ref, sc_ref, sh_ref, mu_ref, wd_ref, mix_ref, hid_ref,
                *, tpb, widths, use_v):
    i = pl.program_id(0)

    def nm(x):
        return _rms(x, g_ref[...]) * (1.0 + sc_ref[...]) + sh_ref[...]

    h = nm(x_ref[...])
    hp = nm(xp_ref[...])[SUBLANE - 1:SUBLANE]
    hp = jnp.where(i % tpb == 0, 0.0, hp)
    rows = lax.broadcasted_iota(jnp.int32, h.shape, 0)
    xx = jnp.where(rows == 0, hp, pltpu.roll(h, 1, 0)) - h

    def mix(j):
        return (h + xx * mu_ref[j:j + 1, :]).astype(BF16)

    mix_ref[0] = mix(0)
    mix_ref[1] = mix(1)
    xv = mix(2)
    mix_ref[2] = xv
    wl, al, gl, vl = widths
    o = 0
    hid_ref[:, o:o + wl] = jnp.tanh(_dot(mix(3), wd_ref[:, o:o + wl])).astype(BF16)
    o += wl
    hid_ref[:, o:o + al] = _dot(mix(4), wd_ref[:, o:o + al]).astype(BF16)
    o += al
    hid_ref[:, o:o + gl] = jax.nn.sigmoid(_dot(mix(5), wd_ref[:, o:o + gl])).astype(BF16)
    o += gl
    if use_v:
        hid_ref[:, o:o + vl] = _dot(xv, wd_ref[:, o:o + vl]).astype(BF16)


def _mix(x, g, mod, layer, mu, wd, widths, use_v, seq):
    n, d = x.shape
    hl = wd.shape[1]
    bm = _blk(seq, 128, SUBLANE)
    tpb = seq // bm
    per8 = bm // SUBLANE
    return pl.pallas_call(
        functools.partial(_mix_kernel, tpb=tpb, widths=widths, use_v=use_v),
        grid=(n // bm,),
        in_specs=[
            pl.BlockSpec((bm, d), lambda i: (i, 0)),
            pl.BlockSpec((SUBLANE, d), lambda i: (jnp.maximum(i * per8 - 1, 0), 0)),
            pl.BlockSpec((1, d), lambda i: (0, 0)),
            _mod_spec(layer, 1, tpb, d, None),
            _mod_spec(layer, 0, tpb, d, None),
            pl.BlockSpec((6, d), lambda i: (0, 0)),
            pl.BlockSpec((d, hl), lambda i: (0, 0)),
        ],
        out_specs=[pl.BlockSpec((3, bm, d), lambda i: (0, i, 0)),
                   pl.BlockSpec((bm, hl), lambda i: (i, 0))],
        out_shape=[jax.ShapeDtypeStruct((3, n, d), BF16),
                   jax.ShapeDtypeStruct((n, hl), BF16)],
        compiler_params=_cparams(("parallel",)),
        name="rwkv_mix",
    )(x, x, g.reshape(1, d), mod, mod, mu, wd)


def _conv_in_kernel(h_ref, wb_ref, wc_ref, wu_ref, cw_ref, o_ref, carry_ref, *, tpb):
    m = pl.program_id(1)
    h = h_ref[...]
    bg = _dot(h, wb_ref[...])
    z = _dot(h, wc_ref[...]) * _dot(h, wu_ref[...])
    bm = z.shape[0]
    prev = jnp.where(m % tpb == 0, 0.0, carry_ref[...])
    carry_ref[...] = z[bm - SUBLANE:bm]
    w0 = cw_ref[0:1, :]
    w1 = cw_ref[1:2, :]
    w2 = cw_ref[2:3, :]
    y = w0 * pltpu.roll(z, 2, 0) + w1 * pltpu.roll(z, 1, 0) + w2 * z
    o_ref[...] = (bg * y).astype(o_ref.dtype)
    zt = z[0:SUBLANE]
    r8 = lax.broadcasted_iota(jnp.int32, zt.shape, 0)
    z1 = jnp.where(r8 < 1, pltpu.roll(prev, 1, 0), pltpu.roll(zt, 1, 0))
    z2 = jnp.where(r8 < 2, pltpu.roll(prev, 2, 0), pltpu.roll(zt, 2, 0))
    yt = w0 * z2 + w1 * z1 + w2 * zt
    o_ref[0:SUBLANE, :] = (bg[0:SUBLANE] * yt).astype(o_ref.dtype)


def _conv_in(h, w_in, conv_w, layer, seq):
    n, d = h.shape
    bm = _blk(seq, 1024, SUBLANE)
    bn = _blk(d, 256, LANE)
    tpb = seq // bm
    nj = d // bn
    return pl.pallas_call(
        functools.partial(_conv_in_kernel, tpb=tpb),
        grid=(nj, n // bm),
        in_specs=[
            pl.BlockSpec((bm, d), lambda j, i: (i, 0)),
            pl.BlockSpec((None, d, bn), lambda j, i: (layer, 0, j)),
            pl.BlockSpec((None, d, bn), lambda j, i: (layer, 0, j + nj)),
            pl.BlockSpec((None, d, bn), lambda j, i: (layer, 0, j + 2 * nj)),
            pl.BlockSpec((None, CONV_WIDTH, bn), lambda j, i: (layer, 0, j)),
        ],
        out_specs=pl.BlockSpec((bm, bn), lambda j, i: (i, j)),
        out_shape=jax.ShapeDtypeStruct((n, d), BF16),
        scratch_shapes=[pltpu.VMEM((SUBLANE, bn), F32)],
        compiler_params=_cparams(("arbitrary", "arbitrary")),
        name="conv_in",
    )(h, w_in, w_in, w_in, conv_w)


def _mm_resid_kernel(a_ref, w_ref, x_ref, gate_ref, o_ref, *, nk):
    if nk == 1:
        o_ref[...] = x_ref[...] + gate_ref[...] * _dot(a_ref[...], w_ref[...])
        return
    k = pl.program_id(2)

    @pl.when(k == 0)
    def _():
        o_ref[...] = jnp.zeros_like(o_ref)

    acc = o_ref[...] + _dot(a_ref[...], w_ref[...])
    o_ref[...] = jnp.where(k == nk - 1, x_ref[...] + gate_ref[...] * acc, acc)


def _mm_resid(a, w, widx, x, mod, layer, gate_chunk, seq):
    n, kdim = a.shape
    d = w.shape[2]
    bm = _blk(seq, 1024, SUBLANE)
    bk = _blk(kdim, 4096 if kdim <= 4096 else 2048, LANE)
    nk = kdim // bk
    bn = _blk(d, 512 if nk == 1 else 1024, LANE)
    tpb = seq // bm
    return pl.pallas_call(
        functools.partial(_mm_resid_kernel, nk=nk),
        grid=(n // bm, d // bn, nk),
        in_specs=[
            pl.BlockSpec((bm, bk), lambda i, j, k: (i, k)),
            pl.BlockSpec((None, bk, bn), lambda i, j, k: (widx, k, j)),
            pl.BlockSpec((bm, bn), lambda i, j, k: (i, j)),
            _mod_spec(layer, gate_chunk, tpb, bn, 1),
        ],
        out_specs=pl.BlockSpec((bm, bn), lambda i, j, k: (i, j)),
        out_shape=jax.ShapeDtypeStruct((n, d), F32),
        compiler_params=_cparams(("parallel", "parallel", "arbitrary")),
        name="mm_resid",
    )(a, w, x, mod)


def _mlp_up_kernel(a_ref, w_ref, o_ref):
    o_ref[...] = jnp.square(jnp.maximum(_dot(a_ref[...], w_ref[...]), 0.0)).astype(o_ref.dtype)


def _mlp_up(a, w, widx):
    n, kdim = a.shape
    f = w.shape[2]
    bm = _blk(n, 1024, SUBLANE)
    bn = _blk(f, 1024, LANE)
    return pl.pallas_call(
        _mlp_up_kernel,
        grid=(n // bm, f // bn),
        in_specs=[pl.BlockSpec((bm, kdim), lambda i, j: (i, 0)),
                  pl.BlockSpec((None, kdim, bn), lambda i, j: (widx, 0, j))],
        out_specs=pl.BlockSpec((bm, bn), lambda i, j: (i, j)),
        out_shape=jax.ShapeDtypeStruct((n, f), BF16),
        compiler_params=_cparams(("parallel", "parallel")),
        name="mlp_up",
    )(a, w)


def _rkv_kernel(a_ref, w_ref, o_ref):
    o_ref[...] = _dot(a_ref[...], w_ref[...])


def _rkv(mixes, w_rkv, widx):
    _, n, d = mixes.shape
    bm = _blk(n, 1024, SUBLANE)
    bn = _blk(d, 1024, LANE)
    return pl.pallas_call(
        _rkv_kernel,
        grid=(3, n // bm, d // bn),
        in_specs=[pl.BlockSpec((None, bm, d), lambda s, i, j: (s, i, 0)),
                  pl.BlockSpec((None, None, d, bn), lambda s, i, j: (widx, s, 0, j))],
        out_specs=pl.BlockSpec((None, bm, bn), lambda s, i, j: (s, i, j)),
        out_shape=jax.ShapeDtypeStruct((3, n, d), F32),
        compiler_params=_cparams(("parallel", "parallel", "parallel")),
        name="rkv_proj",
    )(mixes, w_rkv)


def _prep_kernel(*refs, widths, use_v):
    if use_v:
        (hid_ref, wu_ref, k_ref, v_ref, vf_ref, w0_ref, a0_ref, kk_ref, ka_ref, v0_ref,
         lw_ref, a_ref, g_ref, kkn_ref, km_ref, vm_ref) = refs
    else:
        (hid_ref, wu_ref, k_ref, w0_ref, a0_ref, kk_ref, ka_ref,
         lw_ref, a_ref, g_ref, kkn_ref, km_ref) = refs
    wl, al, gl, vl = widths
    o = 0
    x = w0_ref[...] + _dot(hid_ref[:, o:o + wl], wu_ref[o:o + wl, :])
    o += wl
    sp = jnp.maximum(-x, 0.0) + jnp.log(1.0 + jnp.exp(-jnp.abs(x)))
    lw_ref[...] = -jnp.exp(-sp - 0.5)
    a = jax.nn.sigmoid(a0_ref[...] + _dot(hid_ref[:, o:o + al], wu_ref[o:o + al, :]))
    o += al
    a_ref[...] = a
    g_ref[...] = _dot(hid_ref[:, o:o + gl], wu_ref[o:o + gl, :])
    o += gl
    k = k_ref[...]
    kk = k * kk_ref[...]
    ss = _segsum(kk * kk, _head_ones(LANE))
    kkn_ref[...] = kk * lax.rsqrt(jnp.maximum(ss, 1e-24))
    km_ref[...] = k * (1.0 + (a - 1.0) * ka_ref[...])
    if use_v:
        v = v_ref[...]
        vg = jax.nn.sigmoid(v0_ref[...] + _dot(hid_ref[:, o:o + vl], wu_ref[o:o + vl, :]))
        vm_ref[...] = v + (vf_ref[...] - v) * vg


def _prep(hid, wu, rkv, v_first, w0, a0, k_k, k_a, v0, widths, use_v):
    n, hl = hid.shape
    d = wu.shape[1]
    bm = _blk(n, 512, SUBLANE)
    bn = _blk(d, 1024, LANE)
    row = lambda p: p.reshape(1, d)
    blk = pl.BlockSpec((bm, bn), lambda i, j: (i, j))
    prow = pl.BlockSpec((1, bn), lambda i, j: (0, j))
    ins = [hid, wu, rkv]
    specs = [pl.BlockSpec((bm, hl), lambda i, j: (i, 0)),
             pl.BlockSpec((hl, bn), lambda i, j: (0, j)),
             pl.BlockSpec((None, bm, bn), lambda i, j: (1, i, j))]
    if use_v:
        vf_arr, vf_lead = v_first
        ins += [rkv, vf_arr]
        specs += [pl.BlockSpec((None, bm, bn), lambda i, j: (2, i, j)),
                  pl.BlockSpec((None, bm, bn), lambda i, j: (vf_lead, i, j))]
    ins += [row(w0), row(a0), row(k_k), row(k_a)]
    specs += [prow] * 4
    n_out = 5
    if use_v:
        ins.append(row(v0))
        specs.append(prow)
        n_out = 6
    return pl.pallas_call(
        functools.partial(_prep_kernel, widths=widths, use_v=use_v),
        grid=(n // bm, d // bn),
        in_specs=specs,
        out_specs=[blk] * n_out,
        out_shape=[jax.ShapeDtypeStruct((n, d), F32)] * n_out,
        compiler_params=_cparams(("parallel", "parallel")),
        name="rwkv_prep",
    )(*ins)


def _wkv_chunk(s_prev, ins, consts):
    tri, head_masks, strict_w, incl_w, blk_w, same, eye = consts
    ng = range(len(ins))

    def stack(x):
        return jnp.concatenate([jnp.where(m, x, 0.0) for m in head_masks], axis=0).astype(BF16)

    def bdot(x, y):
        return _dot(x, y).astype(BF16)

    ar, kb_s, v_s, vk_l, vk_r, gl = [], [], [], [], [], []
    for r, lw, k, v, kk, a in ins:
        h1, h2, h3 = _split3(lw)
        cum = _dot(tri, h1) + _dot(tri, h2) + _dot(tri, h3)
        cl = cum[CHUNK - 1:CHUNK]
        gi = jnp.exp(-cum)
        b = kk * a
        dl = jnp.exp(cl - cum)
        ar.append(jnp.concatenate([-(kk * jnp.exp(cum - lw)), r * jnp.exp(cum)], axis=0).astype(BF16))
        kb_s.append(jnp.concatenate([stack(k * gi), stack(b * gi)], axis=0))
        v_s.append(stack(v))
        vk_l.append(v.astype(BF16))
        vk_r.append(jnp.concatenate([k * dl, b * dl], axis=0).astype(BF16))
        gl.append(jnp.exp(cl))

    sc = [_dot_nt(ar[g], kb_s[g]) for g in ng]
    a_ak = [jnp.where(strict_w, sc[g][0:CHUNK, 0:GROUP], 0.0).astype(BF16) for g in ng]
    a_rk = [jnp.where(incl_w, sc[g][CHUNK:2 * CHUNK, 0:GROUP], 0.0).astype(BF16) for g in ng]
    a_rb = [jnp.where(incl_w, sc[g][CHUNK:2 * CHUNK, GROUP:2 * GROUP], 0.0).astype(BF16) for g in ng]
    n_w = [jnp.where(strict_w, sc[g][0:CHUNK, GROUP:2 * GROUP], 0.0) for g in ng]
    dm = [stack(jnp.where(blk_w, n_w[g], 0.0)) for g in ng]
    e = [stack(jnp.where(blk_w, 0.0, n_w[g])) for g in ng]

    ss = [_dot_nt(ar[g], s_prev[g].astype(BF16)) for g in ng]
    pm = [stack(ss[g][0:CHUNK] + _dot(a_ak[g], v_s[g])) for g in ng]
    y0 = [ss[g][CHUNK:2 * CHUNK] + _dot(a_rk[g], v_s[g]) for g in ng]

    p = dm
    td = [eye + dm[g] for g in ng]
    for _ in range(3):
        p = [bdot(p[g], p[g]) for g in ng]
        td = [(td[g].astype(F32) + _dot(td[g], p[g])).astype(BF16) for g in ng]
    m1 = [bdot(td[g], e[g]) for g in ng]
    u = [bdot(td[g], pm[g]) for g in ng]
    m2 = [bdot(m1[g], m1[g]) for g in ng]
    w = [(u[g].astype(F32) + _dot(m2[g], u[g])).astype(BF16) for g in ng]
    sa = [(w[g].astype(F32) + _dot(m1[g], w[g])).astype(BF16) for g in ng]

    y, s_next = [], []
    for g in ng:
        y.append(y0[g] + _dot(a_rb[g], sa[g]))
    for g in ng:
        sa_n = sa[g][0:CHUNK]
        for hh in range(1, GROUP_HEADS):
            sa_n = sa_n + sa[g][hh * CHUNK:(hh + 1) * CHUNK]
        lhs = jnp.concatenate([vk_l[g], sa_n], axis=0)
        upd = _dot_tn(lhs, vk_r[g])
        s_next.append(s_prev[g] * gl[g] + jnp.where(same, upd, 0.0))
    return y, s_next


def _wkv_kernel(r_ref, lw_ref, k_ref, v_ref, kk_ref, a_ref, y_ref, s_ref, *, groups):
    c = pl.program_id(2)

    @pl.when(c == 0)
    def _():
        s_ref[...] = jnp.zeros_like(s_ref)

    ri = lax.broadcasted_iota(jnp.int32, (GROUP, GROUP), 0)
    ci = lax.broadcasted_iota(jnp.int32, (GROUP, GROUP), 1)
    same = (ri // CHUNK) == (ci // CHUNK)
    eye = jnp.where(ri == ci, 1.0, 0.0).astype(BF16)
    wt = lax.broadcasted_iota(jnp.int32, (CHUNK, GROUP), 0)
    ws = lax.broadcasted_iota(jnp.int32, (CHUNK, GROUP), 1) % CHUNK
    strict_w = wt > ws
    incl_w = wt >= ws
    blk_w = (wt // SUB) == (ws // SUB)
    ti = lax.broadcasted_iota(jnp.int32, (CHUNK, CHUNK), 0)
    tj = lax.broadcasted_iota(jnp.int32, (CHUNK, CHUNK), 1)
    tri = jnp.where(ti >= tj, 1.0, 0.0).astype(BF16)
    lane_head = lax.broadcasted_iota(jnp.int32, (CHUNK, GROUP), 1) // HEAD
    head_masks = [lane_head == hh for hh in range(GROUP_HEADS)]
    consts = (tri, head_masks, strict_w, incl_w, blk_w, same, eye)

    cols = [slice(g * GROUP, (g + 1) * GROUP) for g in range(groups)]
    ins = [tuple(ref[:, sl] for ref in (r_ref, lw_ref, k_ref, v_ref, kk_ref, a_ref)) for sl in cols]
    y, s_next = _wkv_chunk([s_ref[g] for g in range(groups)], ins, consts)
    for g, sl in enumerate(cols):
        y_ref[:, sl] = y[g]
        s_ref[g] = s_next[g]


def _wkv(r_src, lw, k, v_src, kk, a, bsz, seq):
    n, d = lw.shape
    groups = _blk(d, WKV_GROUPS * GROUP, GROUP) // GROUP
    gw = groups * GROUP
    nc = seq // CHUNK

    def spec(src):
        arr, lead = src
        if lead is None:
            return arr, pl.BlockSpec((CHUNK, gw), lambda bb, j, c: (bb * nc + c, j))
        return arr, pl.BlockSpec((None, CHUNK, gw), lambda bb, j, c: (lead, bb * nc + c, j))

    ins, specs = zip(*[spec(s) for s in (r_src, (lw, None), (k, None), v_src, (kk, None), (a, None))])
    return pl.pallas_call(
        functools.partial(_wkv_kernel, groups=groups),
        grid=(bsz, d // gw, nc),
        in_specs=list(specs),
        out_specs=pl.BlockSpec((CHUNK, gw), lambda bb, j, c: (bb * nc + c, j)),
        out_shape=jax.ShapeDtypeStruct((n, d), F32),
        scratch_shapes=[pltpu.VMEM((groups, GROUP, GROUP), F32)],
        compiler_params=_cparams(("parallel", "parallel", "arbitrary")),
        name="wkv7",
    )(*ins)


def _post_kernel(y_ref, r_ref, k_ref, v_ref, g_ref, lg_ref, lb_ref, rk_ref, o_ref):
    ones = _head_ones(LANE)
    y = y_ref[...]
    inv = 1.0 / HEAD
    dlt = y - _segsum(y, ones) * inv
    var = _segsum(dlt * dlt, ones) * inv
    yn = dlt * lax.rsqrt(var + GN_EPS) * lg_ref[...] + lb_ref[...]
    bonus = _segsum(r_ref[...] * k_ref[...] * rk_ref[...], ones) * v_ref[...]
    o_ref[...] = ((yn + bonus) * g_ref[...]).astype(o_ref.dtype)


def _post(y, rkv, km, v_src, g, lnx_g, lnx_b, r_k):
    n, d = y.shape
    bm = _blk(n, 512, SUBLANE)
    bn = _blk(d, 1024, LANE)
    blk = pl.BlockSpec((bm, bn), lambda i, j: (i, j))
    prow = pl.BlockSpec((1, bn), lambda i, j: (0, j))
    v_arr, v_lead = v_src
    v_spec = blk if v_lead is None else pl.BlockSpec((None, bm, bn), lambda i, j: (v_lead, i, j))
    return pl.pallas_call(
        _post_kernel,
        grid=(n // bm, d // bn),
        in_specs=[blk, pl.BlockSpec((None, bm, bn), lambda i, j: (0, i, j)), blk, v_spec, blk,
                  prow, prow, prow],
        out_specs=blk,
        out_shape=jax.ShapeDtypeStruct((n, d), BF16),
        compiler_params=_cparams(("parallel", "parallel")),
        name="rwkv_post",
    )(y, rkv, km, v_arr, g, lnx_g.reshape(1, d), lnx_b.reshape(1, d), r_k.reshape(1, d))


def _pad_to(w, axis, mult):
    size = w.shape[axis]
    target = -(-size // mult) * mult
    if target == size:
        return w
    pad = [(0, 0)] * w.ndim
    pad[axis] = (0, target - size)
    return jnp.pad(w, pad)


def kernel(x, c, ada_w, ada_b, ada_emb, ln1_g, ln2_g, mlp_w1, mlp_w2, conv_w_in, conv_w, conv_w_out, rwkv_mu, rwkv_w_rkv, rwkv_w0, rwkv_w1, rwkv_w2, rwkv_a0, rwkv_a1, rwkv_a2, rwkv_g1, rwkv_g2, rwkv_k_k, rwkv_k_a, rwkv_r_k, rwkv_lnx_g, rwkv_lnx_b, rwkv_w_o, rwkv_v0, rwkv_v1, rwkv_v2, final_g):
    bsz, seq, d = x.shape
    depth = ada_emb.shape[0]
    n = bsz * seq
    assert d % GROUP == 0 and seq % CHUNK == 0

    mod = _ada(c, ada_w, ada_b, ada_emb)
    w1_b, w2_b = mlp_w1.astype(BF16), mlp_w2.astype(BF16)
    cin_b, cout_b = conv_w_in.astype(BF16), conv_w_out.astype(BF16)
    rkv_b, wo_b = rwkv_w_rkv.astype(BF16), rwkv_w_o.astype(BF16)
    xs = x.reshape(n, d)
    v_first = None
    for i in range(depth):
        j = i // 2
        if i % 2 == 0:
            h = _norm_mod(xs, ln1_g[i], mod, i, 1, 0, seq)
            gated = _conv_in(h, cin_b, conv_w, j, seq)
            xs = _mm_resid(gated, cout_b, j, xs, mod, i, 2, seq)
        else:
            use_v = v_first is not None
            downs = [rwkv_w1[j], rwkv_a1[j], rwkv_g1[j]]
            ups = [rwkv_w2[j], rwkv_a2[j], rwkv_g2[j]]
            if use_v:
                downs.append(rwkv_v1[j - 1])
                ups.append(rwkv_v2[j - 1])
            downs = [_pad_to(w, 1, LANE) for w in downs]
            ups = [_pad_to(w, 0, LANE) for w in ups]
            widths = tuple(w.shape[1] for w in downs) + ((0,) if not use_v else ())
            wd = jnp.concatenate(downs, axis=1).astype(BF16)
            wu = jnp.concatenate(ups, axis=0).astype(BF16)
            mixes, hid = _mix(xs, ln1_g[i], mod, i, rwkv_mu[j], wd, widths, use_v, seq)
            rkv = _rkv(mixes, rkv_b, j)
            outs = _prep(hid, wu, rkv, v_first, rwkv_w0[j], rwkv_a0[j], rwkv_k_k[j],
                         rwkv_k_a[j], rwkv_v0[j - 1] if use_v else None, widths, use_v)
            if use_v:
                lw, a, g, kkn, km, vm = outs
                v_src = (vm, None)
            else:
                lw, a, g, kkn, km = outs
                v_src = (rkv, 2)
                v_first = (rkv, 2)
            y = _wkv((rkv, 0), lw, km, v_src, kkn, a, bsz, seq)
            gated = _post(y, rkv, km, v_src, g, rwkv_lnx_g[j], rwkv_lnx_b[j], rwkv_r_k[j])
            xs = _mm_resid(gated, wo_b, j, xs, mod, i, 2, seq)
        h = _norm_mod(xs, ln2_g[i], mod, i, 4, 3, seq)
        hidden = _mlp_up(h, w1_b, i)
        xs = _mm_resid(hidden, w2_b, i, xs, mod, i, 5, seq)
    return _final_norm(xs, final_g).reshape(bsz, seq, d)
```

```python
import functools

import jax
import jax.numpy as jnp
from jax import lax
from jax.experimental import pallas as pl
from jax.experimental.pallas import tpu as pltpu

F32 = jnp.float32
BF16 = jnp.bfloat16

HEAD = 64
CHUNK = 64
SUB = 16
GROUP_HEADS = 4
GROUP = GROUP_HEADS * HEAD
WKV_GROUPS = 16
LANE = 128
SUBLANE = 8
ADA_CHUNKS = 6
CONV_WIDTH = 3
RMS_EPS = 1e-5
GN_EPS = 64e-5
VMEM_LIMIT = 56 * 1024 * 1024


def _blk(n, pref, align):
    best = None
    d = align
    while d <= min(n, pref):
        if n % d == 0:
            best = d
        d += align
    return best if best is not None else n


def _cparams(sem):
    return pltpu.CompilerParams(dimension_semantics=sem, vmem_limit_bytes=VMEM_LIMIT)


def _dot(a, b):
    return jnp.dot(a, b, preferred_element_type=F32)


def _dot_nt(a, b):
    return lax.dot_general(a, b, (((1,), (1,)), ((), ())), preferred_element_type=F32)


def _dot_tn(a, b):
    return lax.dot_general(a, b, (((0,), (0,)), ((), ())), preferred_element_type=F32)


def _split3(x):
    h1 = x.astype(BF16)
    r1 = x - h1.astype(F32)
    h2 = r1.astype(BF16)
    h3 = (r1 - h2.astype(F32)).astype(BF16)
    return h1, h2, h3


def _head_ones(n):
    r = lax.broadcasted_iota(jnp.int32, (n, n), 0) // HEAD
    c = lax.broadcasted_iota(jnp.int32, (n, n), 1) // HEAD
    return jnp.where(r == c, 1.0, 0.0).astype(BF16)


def _segsum(x, ones):
    outs = []
    for s in range(x.shape[1] // LANE):
        h1, h2, h3 = _split3(x[:, s * LANE:(s + 1) * LANE])
        outs.append(_dot(h1, ones) + _dot(h2, ones) + _dot(h3, ones))
    return outs[0] if len(outs) == 1 else jnp.concatenate(outs, axis=1)


def _ada_kernel(c_ref, w_ref, b_ref, emb_ref, o_ref, *, depth):
    s = jax.nn.silu(c_ref[...])
    acc = jnp.dot(s, w_ref[...], preferred_element_type=F32,
                  precision=lax.Precision.HIGHEST) + b_ref[...]
    for i in range(depth):
        o_ref[i] = acc + emb_ref[i]


def _ada(c, ada_w, ada_b, ada_emb):
    bsz, d = c.shape
    depth = ada_emb.shape[0]
    n = ada_w.shape[1]
    rows = -(-bsz // SUBLANE) * SUBLANE
    cp = jnp.pad(c, ((0, rows - bsz), (0, 0)))
    bn = _blk(n, 512, LANE)
    out = pl.pallas_call(
        functools.partial(_ada_kernel, depth=depth),
        grid=(n // bn,),
        in_specs=[
            pl.BlockSpec((rows, d), lambda j: (0, 0)),
            pl.BlockSpec((d, bn), lambda j: (0, j)),
            pl.BlockSpec((1, bn), lambda j: (0, j)),
            pl.BlockSpec((depth, 1, bn), lambda j: (0, 0, j)),
        ],
        out_specs=pl.BlockSpec((depth, rows, bn), lambda j: (0, 0, j)),
        out_shape=jax.ShapeDtypeStruct((depth, rows, n), F32),
        compiler_params=_cparams(("parallel",)),
        name="ada_mod",
    )(cp, ada_w, ada_b.reshape(1, n), ada_emb.reshape(depth, 1, n))
    mod = out[:, :bsz].reshape(depth, bsz, ADA_CHUNKS, d)
    return jnp.transpose(mod, (0, 2, 1, 3)).reshape(depth, ADA_CHUNKS, bsz, 1, d)


def _mod_spec(layer, chunk, tpb, bn, col_axis):
    if col_axis is None:
        return pl.BlockSpec((None, None, None, 1, bn), lambda i: (layer, chunk, i // tpb, 0, 0))
    return pl.BlockSpec((None, None, None, 1, bn),
                        lambda i, j, *_: (layer, chunk, i // tpb, 0, j))


def _rms(x, g):
    ms = jnp.mean(x * x, axis=-1, keepdims=True)
    return (x * lax.rsqrt(ms + RMS_EPS)) * g


def _norm_mod_kernel(x_ref, g_ref, sc_ref, sh_ref, o_ref):
    y = _rms(x_ref[...], g_ref[...])
    o_ref[...] = (y * (1.0 + sc_ref[...]) + sh_ref[...]).astype(o_ref.dtype)


def _norm_mod(x, g, mod, layer, sc_chunk, sh_chunk, seq):
    n, d = x.shape
    bm = _blk(seq, 512, SUBLANE)
    tpb = seq // bm
    return pl.pallas_call(
        _norm_mod_kernel,
        grid=(n // bm,),
        in_specs=[
            pl.BlockSpec((bm, d), lambda i: (i, 0)),
            pl.BlockSpec((1, d), lambda i: (0, 0)),
            _mod_spec(layer, sc_chunk, tpb, d, None),
            _mod_spec(layer, sh_chunk, tpb, d, None),
        ],
        out_specs=pl.BlockSpec((bm, d), lambda i: (i, 0)),
        out_shape=jax.ShapeDtypeStruct((n, d), BF16),
        compiler_params=_cparams(("parallel",)),
        name="norm_mod",
    )(x, g.reshape(1, d), mod, mod)


def _final_norm_kernel(x_ref, g_ref, o_ref):
    o_ref[...] = _rms(x_ref[...], g_ref[...])


def _final_norm(x, g):
    n, d = x.shape
    bm = _blk(n, 512, SUBLANE)
    return pl.pallas_call(
        _final_norm_kernel,
        grid=(n // bm,),
        in_specs=[pl.BlockSpec((bm, d), lambda i: (i, 0)),
                  pl.BlockSpec((1, d), lambda i: (0, 0))],
        out_specs=pl.BlockSpec((bm, d), lambda i: (i, 0)),
        out_shape=jax.ShapeDtypeStruct((n, d), F32),
        compiler_params=_cparams(("parallel",)),
        name="final_norm",
    )(x, g.reshape(1, d))


def _mix_kernel(x_ref, xp_ref, g_ref, sc_ref, sh_ref, mu_ref, wd_ref, mix_ref, hid_ref,
                *, tpb, widths, use_v):
    i = pl.program_id(0)

    def nm(x):
        return _rms(x, g_ref[...]) * (1.0 + sc_ref[...]) + sh_ref[...]

    h = nm(x_ref[...])
    hp = nm(xp_ref[...])[SUBLANE - 1:SUBLANE]
    hp = jnp.where(i % tpb == 0, 0.0, hp)
    rows = lax.broadcasted_iota(jnp.int32, h.shape, 0)
    xx = jnp.where(rows == 0, hp, pltpu.roll(h, 1, 0)) - h

    def mix(j):
        return (h + xx * mu_ref[j:j + 1, :]).astype(BF16)

    mix_ref[0] = mix(0)
    mix_ref[1] = mix(1)
    xv = mix(2)
    mix_ref[2] = xv
    wl, al, gl, vl = widths
    o = 0
    hid_ref[:, o:o + wl] = jnp.tanh(_dot(mix(3), wd_ref[:, o:o + wl])).astype(BF16)
    o += wl
    hid_ref[:, o:o + al] = _dot(mix(4), wd_ref[:, o:o + al]).astype(BF16)
    o += al
    hid_ref[:, o:o + gl] = jax.nn.sigmoid(_dot(mix(5), wd_ref[:, o:o + gl])).astype(BF16)
    o += gl
    if use_v:
        hid_ref[:, o:o + vl] = _dot(xv, wd_ref[:, o:o + vl]).astype(BF16)


def _mix(x, g, mod, layer, mu, wd, widths, use_v, seq):
    n, d = x.shape
    hl = wd.shape[1]
    bm = _blk(seq, 128, SUBLANE)
    tpb = seq // bm
    per8 = bm // SUBLANE
    return pl.pallas_call(
        functools.partial(_mix_kernel, tpb=tpb, widths=widths, use_v=use_v),
        grid=(n // bm,),
        in_specs=[
            pl.BlockSpec((bm, d), lambda i: (i, 0)),
            pl.BlockSpec((SUBLANE, d), lambda i: (jnp.maximum(i * per8 - 1, 0), 0)),
            pl.BlockSpec((1, d), lambda i: (0, 0)),
            _mod_spec(layer, 1, tpb, d, None),
            _mod_spec(layer, 0, tpb, d, None),
            pl.BlockSpec((6, d), lambda i: (0, 0)),
            pl.BlockSpec((d, hl), lambda i: (0, 0)),
        ],
        out_specs=[pl.BlockSpec((3, bm, d), lambda i: (0, i, 0)),
                   pl.BlockSpec((bm, hl), lambda i: (i, 0))],
        out_shape=[jax.ShapeDtypeStruct((3, n, d), BF16),
                   jax.ShapeDtypeStruct((n, hl), BF16)],
        compiler_params=_cparams(("parallel",)),
        name="rwkv_mix",
    )(x, x, g.reshape(1, d), mod, mod, mu, wd)


def _conv_in_kernel(h_ref, wb_ref, wc_ref, wu_ref, cw_ref, o_ref, carry_ref, w_ref, *, tpb):
    m = pl.program_id(1)

    @pl.when(m == 0)
    def _():
        w_ref[0] = wb_ref[...].astype(BF16)
        w_ref[1] = wc_ref[...].astype(BF16)
        w_ref[2] = wu_ref[...].astype(BF16)

    h = h_ref[...]
    bg = _dot(h, w_ref[0])
    z = _dot(h, w_ref[1]) * _dot(h, w_ref[2])
    bm = z.shape[0]
    prev = jnp.where(m % tpb == 0, 0.0, carry_ref[...])
    carry_ref[...] = z[bm - SUBLANE:bm]
    w0 = cw_ref[0:1, :]
    w1 = cw_ref[1:2, :]
    w2 = cw_ref[2:3, :]
    y = w0 * pltpu.roll(z, 2, 0) + w1 * pltpu.roll(z, 1, 0) + w2 * z
    o_ref[...] = (bg * y).astype(o_ref.dtype)
    zt = z[0:SUBLANE]
    r8 = lax.broadcasted_iota(jnp.int32, zt.shape, 0)
    z1 = jnp.where(r8 < 1, pltpu.roll(prev, 1, 0), pltpu.roll(zt, 1, 0))
    z2 = jnp.where(r8 < 2, pltpu.roll(prev, 2, 0), pltpu.roll(zt, 2, 0))
    yt = w0 * z2 + w1 * z1 + w2 * zt
    o_ref[0:SUBLANE, :] = (bg[0:SUBLANE] * yt).astype(o_ref.dtype)


def _conv_in(h, w_in, conv_w, layer, seq):
    n, d = h.shape
    bm = _blk(seq, 1024, SUBLANE)
    bn = _blk(d, 256, LANE)
    tpb = seq // bm
    nj = d // bn
    return pl.pallas_call(
        functools.partial(_conv_in_kernel, tpb=tpb),
        grid=(nj, n // bm),
        in_specs=[
            pl.BlockSpec((bm, d), lambda j, i: (i, 0)),
            pl.BlockSpec((None, d, bn), lambda j, i: (layer, 0, j)),
            pl.BlockSpec((None, d, bn), lambda j, i: (layer, 0, j + nj)),
            pl.BlockSpec((None, d, bn), lambda j, i: (layer, 0, j + 2 * nj)),
            pl.BlockSpec((None, CONV_WIDTH, bn), lambda j, i: (layer, 0, j)),
        ],
        out_specs=pl.BlockSpec((bm, bn), lambda j, i: (i, j)),
        out_shape=jax.ShapeDtypeStruct((n, d), BF16),
        scratch_shapes=[pltpu.VMEM((SUBLANE, bn), F32), pltpu.VMEM((3, d, bn), BF16)],
        compiler_params=_cparams(("arbitrary", "arbitrary")),
        name="conv_in",
    )(h, w_in, w_in, w_in, conv_w)


def _mm_resid_kernel(a_ref, w_ref, x_ref, gate_ref, o_ref, *, nk):
    if nk == 1:
        o_ref[...] = x_ref[...] + gate_ref[...] * _dot(a_ref[...], w_ref[...])
        return
    k = pl.program_id(2)

    @pl.when(k == 0)
    def _():
        o_ref[...] = jnp.zeros_like(o_ref)

    acc = o_ref[...] + _dot(a_ref[...], w_ref[...])
    o_ref[...] = jnp.where(k == nk - 1, x_ref[...] + gate_ref[...] * acc, acc)


def _mm_resid(a, w, widx, x, mod, layer, gate_chunk, seq):
    n, kdim = a.shape
    d = w.shape[2]
    bm = _blk(seq, 1024, SUBLANE)
    bk = _blk(kdim, 4096 if kdim <= 4096 else 2048, LANE)
    nk = kdim // bk
    bn = _blk(d, 512 if nk == 1 else 1024, LANE)
    tpb = seq // bm
    return pl.pallas_call(
        functools.partial(_mm_resid_kernel, nk=nk),
        grid=(n // bm, d // bn, nk),
        in_specs=[
            pl.BlockSpec((bm, bk), lambda i, j, k: (i, k)),
            pl.BlockSpec((None, bk, bn), lambda i, j, k: (widx, k, j)),
            pl.BlockSpec((bm, bn), lambda i, j, k: (i, j)),
            _mod_spec(layer, gate_chunk, tpb, bn, 1),
        ],
        out_specs=pl.BlockSpec((bm, bn), lambda i, j, k: (i, j)),
        out_shape=jax.ShapeDtypeStruct((n, d), F32),
        compiler_params=_cparams(("parallel", "parallel", "arbitrary")),
        name="mm_resid",
    )(a, w, x, mod)


def _mlp_up_kernel(a_ref, w_ref, o_ref):
    o_ref[...] = jnp.square(jnp.maximum(_dot(a_ref[...], w_ref[...]), 0.0)).astype(o_ref.dtype)


def _mlp_up(a, w, widx):
    n, kdim = a.shape
    f = w.shape[2]
    bm = _blk(n, 1024, SUBLANE)
    bn = _blk(f, 1024, LANE)
    return pl.pallas_call(
        _mlp_up_kernel,
        grid=(n // bm, f // bn),
        in_specs=[pl.BlockSpec((bm, kdim), lambda i, j: (i, 0)),
                  pl.BlockSpec((None, kdim, bn), lambda i, j: (widx, 0, j))],
        out_specs=pl.BlockSpec((bm, bn), lambda i, j: (i, j)),
        out_shape=jax.ShapeDtypeStruct((n, f), BF16),
        compiler_params=_cparams(("parallel", "parallel")),
        name="mlp_up",
    )(a, w)


def _rkv_kernel(a_ref, w_ref, o_ref):
    o_ref[...] = _dot(a_ref[...], w_ref[...])


def _rkv(mixes, w_rkv, widx):
    _, n, d = mixes.shape
    bm = _blk(n, 1024, SUBLANE)
    bn = _blk(d, 1024, LANE)
    return pl.pallas_call(
        _rkv_kernel,
        grid=(3, n // bm, d // bn),
        in_specs=[pl.BlockSpec((None, bm, d), lambda s, i, j: (s, i, 0)),
                  pl.BlockSpec((None, None, d, bn), lambda s, i, j: (widx, s, 0, j))],
        out_specs=pl.BlockSpec((None, bm, bn), lambda s, i, j: (s, i, j)),
        out_shape=jax.ShapeDtypeStruct((3, n, d), F32),
        compiler_params=_cparams(("parallel", "parallel", "parallel")),
        name="rkv_proj",
    )(mixes, w_rkv)


def _prep_kernel(*refs, widths, use_v):
    if use_v:
        (hid_ref, wu_ref, k_ref, v_ref, vf_ref, w0_ref, a0_ref, kk_ref, ka_ref, v0_ref,
         lw_ref, a_ref, g_ref, kkn_ref, km_ref, vm_ref) = refs
    else:
        (hid_ref, wu_ref, k_ref, w0_ref, a0_ref, kk_ref, ka_ref,
         lw_ref, a_ref, g_ref, kkn_ref, km_ref) = refs
    wl, al, gl, vl = widths
    o = 0
    x = w0_ref[...] + _dot(hid_ref[:, o:o + wl], wu_ref[o:o + wl, :])
    o += wl
    sp = jnp.maximum(-x, 0.0) + jnp.log(1.0 + jnp.exp(-jnp.abs(x)))
    lw_ref[...] = -jnp.exp(-sp - 0.5)
    a = jax.nn.sigmoid(a0_ref[...] + _dot(hid_ref[:, o:o + al], wu_ref[o:o + al, :]))
    o += al
    a_ref[...] = a
    g_ref[...] = _dot(hid_ref[:, o:o + gl], wu_ref[o:o + gl, :])
    o += gl
    k = k_ref[...]
    kk = k * kk_ref[...]
    ss = _segsum(kk * kk, _head_ones(LANE))
    kkn_ref[...] = kk * lax.rsqrt(jnp.maximum(ss, 1e-24))
    km_ref[...] = k * (1.0 + (a - 1.0) * ka_ref[...])
    if use_v:
        v = v_ref[...]
        vg = jax.nn.sigmoid(v0_ref[...] + _dot(hid_ref[:, o:o + vl], wu_ref[o:o + vl, :]))
        vm_ref[...] = v + (vf_ref[...] - v) * vg


def _prep(hid, wu, rkv, v_first, w0, a0, k_k, k_a, v0, widths, use_v):
    n, hl = hid.shape
    d = wu.shape[1]
    bm = _blk(n, 512, SUBLANE)
    bn = _blk(d, 1024, LANE)
    row = lambda p: p.reshape(1, d)
    blk = pl.BlockSpec((bm, bn), lambda i, j: (i, j))
    prow = pl.BlockSpec((1, bn), lambda i, j: (0, j))
    ins = [hid, wu, rkv]
    specs = [pl.BlockSpec((bm, hl), lambda i, j: (i, 0)),
             pl.BlockSpec((hl, bn), lambda i, j: (0, j)),
             pl.BlockSpec((None, bm, bn), lambda i, j: (1, i, j))]
    if use_v:
        vf_arr, vf_lead = v_first
        ins += [rkv, vf_arr]
        specs += [pl.BlockSpec((None, bm, bn), lambda i, j: (2, i, j)),
                  pl.BlockSpec((None, bm, bn), lambda i, j: (vf_lead, i, j))]
    ins += [row(w0), row(a0), row(k_k), row(k_a)]
    specs += [prow] * 4
    n_out = 5
    if use_v:
        ins.append(row(v0))
        specs.append(prow)
        n_out = 6
    return pl.pallas_call(
        functools.partial(_prep_kernel, widths=widths, use_v=use_v),
        grid=(n // bm, d // bn),
        in_specs=specs,
        out_specs=[blk] * n_out,
        out_shape=[jax.ShapeDtypeStruct((n, d), F32)] * n_out,
        compiler_params=_cparams(("parallel", "parallel")),
        name="rwkv_prep",
    )(*ins)


def _wkv_chunk(s_prev, ins, consts):
    tri, head_masks, strict_w, incl_w, blk_w, same, eye = consts
    ng = range(len(ins))

    def stack(x):
        return jnp.concatenate([jnp.where(m, x, 0.0) for m in head_masks], axis=0).astype(BF16)

    def bdot(x, y):
        return _dot(x, y).astype(BF16)

    ar, kb_s, v_s, vk_l, vk_r, gl = [], [], [], [], [], []
    for r, lw, k, v, kk, a in ins:
        h1, h2, h3 = _split3(lw)
        cum = _dot(tri, h1) + _dot(tri, h2) + _dot(tri, h3)
        cl = cum[CHUNK - 1:CHUNK]
        gi = jnp.exp(-cum)
        b = kk * a
        dl = jnp.exp(cl - cum)
        ar.append(jnp.concatenate([-(kk * jnp.exp(cum - lw)), r * jnp.exp(cum)], axis=0).astype(BF16))
        kb_s.append(jnp.concatenate([stack(k * gi), stack(b * gi)], axis=0))
        v_s.append(stack(v))
        vk_l.append(v.astype(BF16))
        vk_r.append(jnp.concatenate([k * dl, b * dl], axis=0).astype(BF16))
        gl.append(jnp.exp(cl))

    sc = [_dot_nt(ar[g], kb_s[g]) for g in ng]
    a_ak = [jnp.where(strict_w, sc[g][0:CHUNK, 0:GROUP], 0.0).astype(BF16) for g in ng]
    a_rk = [jnp.where(incl_w, sc[g][CHUNK:2 * CHUNK, 0:GROUP], 0.0).astype(BF16) for g in ng]
    a_rb = [jnp.where(incl_w, sc[g][CHUNK:2 * CHUNK, GROUP:2 * GROUP], 0.0).astype(BF16) for g in ng]
    n_w = [jnp.where(strict_w, sc[g][0:CHUNK, GROUP:2 * GROUP], 0.0) for g in ng]
    dm = [stack(jnp.where(blk_w, n_w[g], 0.0)) for g in ng]
    e = [stack(jnp.where(blk_w, 0.0, n_w[g])) for g in ng]

    ss = [_dot_nt(ar[g], s_prev[g].astype(BF16)) for g in ng]
    pm = [stack(ss[g][0:CHUNK] + _dot(a_ak[g], v_s[g])) for g in ng]
    y0 = [ss[g][CHUNK:2 * CHUNK] + _dot(a_rk[g], v_s[g]) for g in ng]

    p = dm
    td = [eye + dm[g] for g in ng]
    for _ in range(3):
        p = [bdot(p[g], p[g]) for g in ng]
        td = [(td[g].astype(F32) + _dot(td[g], p[g])).astype(BF16) for g in ng]
    m1 = [bdot(td[g], e[g]) for g in ng]
    u = [bdot(td[g], pm[g]) for g in ng]
    m2 = [bdot(m1[g], m1[g]) for g in ng]
    w = [(u[g].astype(F32) + _dot(m2[g], u[g])).astype(BF16) for g in ng]
    sa = [(w[g].astype(F32) + _dot(m1[g], w[g])).astype(BF16) for g in ng]

    y, s_next = [], []
    for g in ng:
        y.append(y0[g] + _dot(a_rb[g], sa[g]))
    for g in ng:
        sa_n = sa[g][0:CHUNK]
        for hh in range(1, GROUP_HEADS):
            sa_n = sa_n + sa[g][hh * CHUNK:(hh + 1) * CHUNK]
        lhs = jnp.concatenate([vk_l[g], sa_n], axis=0)
        upd = _dot_tn(lhs, vk_r[g])
        s_next.append(s_prev[g] * gl[g] + jnp.where(same, upd, 0.0))
    return y, s_next


def _wkv_kernel(r_ref, lw_ref, k_ref, v_ref, kk_ref, a_ref, y_ref, s_ref, *, groups):
    c = pl.program_id(2)

    @pl.when(c == 0)
    def _():
        s_ref[...] = jnp.zeros_like(s_ref)

    ri = lax.broadcasted_iota(jnp.int32, (GROUP, GROUP), 0)
    ci = lax.broadcasted_iota(jnp.int32, (GROUP, GROUP), 1)
    same = (ri // CHUNK) == (ci // CHUNK)
    eye = jnp.where(ri == ci, 1.0, 0.0).astype(BF16)
    wt = lax.broadcasted_iota(jnp.int32, (CHUNK, GROUP), 0)
    ws = lax.broadcasted_iota(jnp.int32, (CHUNK, GROUP), 1) % CHUNK
    strict_w = wt > ws
    incl_w = wt >= ws
    blk_w = (wt // SUB) == (ws // SUB)
    ti = lax.broadcasted_iota(jnp.int32, (CHUNK, CHUNK), 0)
    tj = lax.broadcasted_iota(jnp.int32, (CHUNK, CHUNK), 1)
    tri = jnp.where(ti >= tj, 1.0, 0.0).astype(BF16)
    lane_head = lax.broadcasted_iota(jnp.int32, (CHUNK, GROUP), 1) // HEAD
    head_masks = [lane_head == hh for hh in range(GROUP_HEADS)]
    consts = (tri, head_masks, strict_w, incl_w, blk_w, same, eye)

    cols = [slice(g * GROUP, (g + 1) * GROUP) for g in range(groups)]
    ins = [tuple(ref[:, sl] for ref in (r_ref, lw_ref, k_ref, v_ref, kk_ref, a_ref)) for sl in cols]
    y, s_next = _wkv_chunk([s_ref[g] for g in range(groups)], ins, consts)
    for g, sl in enumerate(cols):
        y_ref[:, sl] = y[g]
        s_ref[g] = s_next[g]


def _wkv(r_src, lw, k, v_src, kk, a, bsz, seq):
    n, d = lw.shape
    groups = _blk(d, WKV_GROUPS * GROUP, GROUP) // GROUP
    gw = groups * GROUP
    nc = seq // CHUNK

    def spec(src):
        arr, lead = src
        if lead is None:
            return arr, pl.BlockSpec((CHUNK, gw), lambda bb, j, c: (bb * nc + c, j))
        return arr, pl.BlockSpec((None, CHUNK, gw), lambda bb, j, c: (lead, bb * nc + c, j))

    ins, specs = zip(*[spec(s) for s in (r_src, (lw, None), (k, None), v_src, (kk, None), (a, None))])
    return pl.pallas_call(
        functools.partial(_wkv_kernel, groups=groups),
        grid=(bsz, d // gw, nc),
        in_specs=list(specs),
        out_specs=pl.BlockSpec((CHUNK, gw), lambda bb, j, c: (bb * nc + c, j)),
        out_shape=jax.ShapeDtypeStruct((n, d), F32),
        scratch_shapes=[pltpu.VMEM((groups, GROUP, GROUP), F32)],
        compiler_params=_cparams(("parallel", "parallel", "arbitrary")),
        name="wkv7",
    )(*ins)


def _post_kernel(y_ref, r_ref, k_ref, v_ref, g_ref, lg_ref, lb_ref, rk_ref, o_ref):
    ones = _head_ones(LANE)
    y = y_ref[...]
    inv = 1.0 / HEAD
    dlt = y - _segsum(y, ones) * inv
    var = _segsum(dlt * dlt, ones) * inv
    yn = dlt * lax.rsqrt(var + GN_EPS) * lg_ref[...] + lb_ref[...]
    bonus = _segsum(r_ref[...] * k_ref[...] * rk_ref[...], ones) * v_ref[...]
    o_ref[...] = ((yn + bonus) * g_ref[...]).astype(o_ref.dtype)


def _post(y, rkv, km, v_src, g, lnx_g, lnx_b, r_k):
    n, d = y.shape
    bm = _blk(n, 512, SUBLANE)
    bn = _blk(d, 1024, LANE)
    blk = pl.BlockSpec((bm, bn), lambda i, j: (i, j))
    prow = pl.BlockSpec((1, bn), lambda i, j: (0, j))
    v_arr, v_lead = v_src
    v_spec = blk if v_lead is None else pl.BlockSpec((None, bm, bn), lambda i, j: (v_lead, i, j))
    return pl.pallas_call(
        _post_kernel,
        grid=(n // bm, d // bn),
        in_specs=[blk, pl.BlockSpec((None, bm, bn), lambda i, j: (0, i, j)), blk, v_spec, blk,
                  prow, prow, prow],
        out_specs=blk,
        out_shape=jax.ShapeDtypeStruct((n, d), BF16),
        compiler_params=_cparams(("parallel", "parallel")),
        name="rwkv_post",
    )(y, rkv, km, v_arr, g, lnx_g.reshape(1, d), lnx_b.reshape(1, d), r_k.reshape(1, d))


def _pad_to(w, axis, mult):
    size = w.shape[axis]
    target = -(-size // mult) * mult
    if target == size:
        return w
    pad = [(0, 0)] * w.ndim
    pad[axis] = (0, target - size)
    return jnp.pad(w, pad)


def kernel(x, c, ada_w, ada_b, ada_emb, ln1_g, ln2_g, mlp_w1, mlp_w2, conv_w_in, conv_w, conv_w_out, rwkv_mu, rwkv_w_rkv, rwkv_w0, rwkv_w1, rwkv_w2, rwkv_a0, rwkv_a1, rwkv_a2, rwkv_g1, rwkv_g2, rwkv_k_k, rwkv_k_a, rwkv_r_k, rwkv_lnx_g, rwkv_lnx_b, rwkv_w_o, rwkv_v0, rwkv_v1, rwkv_v2, final_g):
    bsz, seq, d = x.shape
    depth = ada_emb.shape[0]
    n = bsz * seq
    assert d % GROUP == 0 and seq % CHUNK == 0

    mod = _ada(c, ada_w, ada_b, ada_emb)
    w1_b, w2_b = mlp_w1.astype(BF16), mlp_w2.astype(BF16)
    cout_b = conv_w_out.astype(BF16)
    rkv_b, wo_b = rwkv_w_rkv.astype(BF16), rwkv_w_o.astype(BF16)
    xs = x.reshape(n, d)
    v_first = None
    for i in range(depth):
        j = i // 2
        if i % 2 == 0:
            h = _norm_mod(xs, ln1_g[i], mod, i, 1, 0, seq)
            gated = _conv_in(h, conv_w_in, conv_w, j, seq)
            xs = _mm_resid(gated, cout_b, j, xs, mod, i, 2, seq)
        else:
            use_v = v_first is not None
            downs = [rwkv_w1[j], rwkv_a1[j], rwkv_g1[j]]
            ups = [rwkv_w2[j], rwkv_a2[j], rwkv_g2[j]]
            if use_v:
                downs.append(rwkv_v1[j - 1])
                ups.append(rwkv_v2[j - 1])
            downs = [_pad_to(w, 1, LANE) for w in downs]
            ups = [_pad_to(w, 0, LANE) for w in ups]
            widths = tuple(w.shape[1] for w in downs) + ((0,) if not use_v else ())
            wd = jnp.concatenate(downs, axis=1).astype(BF16)
            wu = jnp.concatenate(ups, axis=0).astype(BF16)
            mixes, hid = _mix(xs, ln1_g[i], mod, i, rwkv_mu[j], wd, widths, use_v, seq)
            rkv = _rkv(mixes, rkv_b, j)
            outs = _prep(hid, wu, rkv, v_first, rwkv_w0[j], rwkv_a0[j], rwkv_k_k[j],
                         rwkv_k_a[j], rwkv_v0[j - 1] if use_v else None, widths, use_v)
            if use_v:
                lw, a, g, kkn, km, vm = outs
                v_src = (vm, None)
            else:
                lw, a, g, kkn, km = outs
                v_src = (rkv, 2)
                v_first = (rkv, 2)
            y = _wkv((rkv, 0), lw, km, v_src, kkn, a, bsz, seq)
            gated = _post(y, rkv, km, v_src, g, rwkv_lnx_g[j], rwkv_lnx_b[j], rwkv_r_k[j])
            xs = _mm_resid(gated, wo_b, j, xs, mod, i, 2, seq)
        h = _norm_mod(xs, ln2_g[i], mod, i, 4, 3, seq)
        hidden = _mlp_up(h, w1_b, i)
        xs = _mm_resid(hidden, w2_b, i, xs, mod, i, 5, seq)
    return _final_norm(xs, final_g).reshape(bsz, seq, d)
```

```python
import functools

import jax
import jax.numpy as jnp
from jax import lax
from jax.experimental import pallas as pl
from jax.experimental.pallas import tpu as pltpu

F32 = jnp.float32
BF16 = jnp.bfloat16

HEAD = 64
CHUNK = 64
SUB = 16
GROUP_HEADS = 4
GROUP = GROUP_HEADS * HEAD
WKV_GROUPS = 16
LANE = 128
SUBLANE = 8
ADA_CHUNKS = 6
CONV_WIDTH = 3
RMS_EPS = 1e-5
GN_EPS = 64e-5
VMEM_LIMIT = 56 * 1024 * 1024


def _blk(n, pref, align):
    best = None
    d = align
    while d <= min(n, pref):
        if n % d == 0:
            best = d
        d += align
    return best if best is not None else n


def _cparams(sem):
    return pltpu.CompilerParams(dimension_semantics=sem, vmem_limit_bytes=VMEM_LIMIT)


def _dot(a, b):
    return jnp.dot(a, b, preferred_element_type=F32)


def _dot_nt(a, b):
    return lax.dot_general(a, b, (((1,), (1,)), ((), ())), preferred_element_type=F32)


def _dot_tn(a, b):
    return lax.dot_general(a, b, (((0,), (0,)), ((), ())), preferred_element_type=F32)


def _split3(x):
    h1 = x.astype(BF16)
    r1 = x - h1.astype(F32)
    h2 = r1.astype(BF16)
    h3 = (r1 - h2.astype(F32)).astype(BF16)
    return h1, h2, h3


def _head_ones(n):
    r = lax.broadcasted_iota(jnp.int32, (n, n), 0) // HEAD
    c = lax.broadcasted_iota(jnp.int32, (n, n), 1) // HEAD
    return jnp.where(r == c, 1.0, 0.0).astype(BF16)


def _segsum(x, ones):
    outs = []
    for s in range(x.shape[1] // LANE):
        h1, h2, h3 = _split3(x[:, s * LANE:(s + 1) * LANE])
        outs.append(_dot(h1, ones) + _dot(h2, ones) + _dot(h3, ones))
    return outs[0] if len(outs) == 1 else jnp.concatenate(outs, axis=1)


def _ada_kernel(c_ref, w_ref, b_ref, emb_ref, o_ref, *, depth):
    s = jax.nn.silu(c_ref[...])
    acc = jnp.dot(s, w_ref[...], preferred_element_type=F32,
                  precision=lax.Precision.HIGHEST) + b_ref[...]
    for i in range(depth):
        o_ref[i] = acc + emb_ref[i]


def _ada(c, ada_w, ada_b, ada_emb):
    bsz, d = c.shape
    depth = ada_emb.shape[0]
    n = ada_w.shape[1]
    rows = -(-bsz // SUBLANE) * SUBLANE
    cp = jnp.pad(c, ((0, rows - bsz), (0, 0)))
    bn = _blk(n, 512, LANE)
    out = pl.pallas_call(
        functools.partial(_ada_kernel, depth=depth),
        grid=(n // bn,),
        in_specs=[
            pl.BlockSpec((rows, d), lambda j: (0, 0)),
            pl.BlockSpec((d, bn), lambda j: (0, j)),
            pl.BlockSpec((1, bn), lambda j: (0, j)),
            pl.BlockSpec((depth, 1, bn), lambda j: (0, 0, j)),
        ],
        out_specs=pl.BlockSpec((depth, rows, bn), lambda j: (0, 0, j)),
        out_shape=jax.ShapeDtypeStruct((depth, rows, n), F32),
        compiler_params=_cparams(("parallel",)),
        name="ada_mod",
    )(cp, ada_w, ada_b.reshape(1, n), ada_emb.reshape(depth, 1, n))
    mod = out[:, :bsz].reshape(depth, bsz, ADA_CHUNKS, d)
    return jnp.transpose(mod, (0, 2, 1, 3)).reshape(depth, ADA_CHUNKS, bsz, 1, d)


def _mod_spec(layer, chunk, tpb, bn, col_axis):
    if col_axis is None:
        return pl.BlockSpec((None, None, None, 1, bn), lambda i: (layer, chunk, i // tpb, 0, 0))
    return pl.BlockSpec((None, None, None, 1, bn),
                        lambda i, j, *_: (layer, chunk, i // tpb, 0, j))


def _rms(x, g):
    ms = jnp.mean(x * x, axis=-1, keepdims=True)
    return (x * lax.rsqrt(ms + RMS_EPS)) * g


def _norm_mod_kernel(x_ref, g_ref, sc_ref, sh_ref, o_ref):
    y = _rms(x_ref[...], g_ref[...])
    o_ref[...] = (y * (1.0 + sc_ref[...]) + sh_ref[...]).astype(o_ref.dtype)


def _norm_mod(x, g, mod, layer, sc_chunk, sh_chunk, seq):
    n, d = x.shape
    bm = _blk(seq, 512, SUBLANE)
    tpb = seq // bm
    return pl.pallas_call(
        _norm_mod_kernel,
        grid=(n // bm,),
        in_specs=[
            pl.BlockSpec((bm, d), lambda i: (i, 0)),
            pl.BlockSpec((1, d), lambda i: (0, 0)),
            _mod_spec(layer, sc_chunk, tpb, d, None),
            _mod_spec(layer, sh_chunk, tpb, d, None),
        ],
        out_specs=pl.BlockSpec((bm, d), lambda i: (i, 0)),
        out_shape=jax.ShapeDtypeStruct((n, d), BF16),
        compiler_params=_cparams(("parallel",)),
        name="norm_mod",
    )(x, g.reshape(1, d), mod, mod)


def _final_norm_kernel(x_ref, g_ref, o_ref):
    o_ref[...] = _rms(x_ref[...], g_ref[...])


def _final_norm(x, g):
    n, d = x.shape
    bm = _blk(n, 512, SUBLANE)
    return pl.pallas_call(
        _final_norm_kernel,
        grid=(n // bm,),
        in_specs=[pl.BlockSpec((bm, d), lambda i: (i, 0)),
                  pl.BlockSpec((1, d), lambda i: (0, 0))],
        out_specs=pl.BlockSpec((bm, d), lambda i: (i, 0)),
        out_shape=jax.ShapeDtypeStruct((n, d), F32),
        compiler_params=_cparams(("parallel",)),
        name="final_norm",
    )(x, g.reshape(1, d))


def _mix_kernel(x_ref, xp_ref, g_ref, sc_ref, sh_ref, mu_ref, wd_ref, mix_ref, hid_ref,
                *, tpb, widths, use_v):
    i = pl.program_id(0)

    def nm(x):
        return _rms(x, g_ref[...]) * (1.0 + sc_ref[...]) + sh_ref[...]

    h = nm(x_ref[...])
    hp = nm(xp_ref[...])[SUBLANE - 1:SUBLANE]
    hp = jnp.where(i % tpb == 0, 0.0, hp)
    rows = lax.broadcasted_iota(jnp.int32, h.shape, 0)
    xx = jnp.where(rows == 0, hp, pltpu.roll(h, 1, 0)) - h

    def mix(j):
        return (h + xx * mu_ref[j:j + 1, :]).astype(BF16)

    mix_ref[0] = mix(0)
    mix_ref[1] = mix(1)
    xv = mix(2)
    mix_ref[2] = xv
    wl, al, gl, vl = widths
    o = 0
    hid_ref[:, o:o + wl] = jnp.tanh(_dot(mix(3), wd_ref[:, o:o + wl])).astype(BF16)
    o += wl
    hid_ref[:, o:o + al] = _dot(mix(4), wd_ref[:, o:o + al]).astype(BF16)
    o += al
    hid_ref[:, o:o + gl] = jax.nn.sigmoid(_dot(mix(5), wd_ref[:, o:o + gl])).astype(BF16)
    o += gl
    if use_v:
        hid_ref[:, o:o + vl] = _dot(xv, wd_ref[:, o:o + vl]).astype(BF16)


def _mix(x, g, mod, layer, mu, wd, widths, use_v, seq):
    n, d = x.shape
    hl = wd.shape[1]
    bm = _blk(seq, 128, SUBLANE)
    tpb = seq // bm
    per8 = bm // SUBLANE
    return pl.pallas_call(
        functools.partial(_mix_kernel, tpb=tpb, widths=widths, use_v=use_v),
        grid=(n // bm,),
        in_specs=[
            pl.BlockSpec((bm, d), lambda i: (i, 0)),
            pl.BlockSpec((SUBLANE, d), lambda i: (jnp.maximum(i * per8 - 1, 0), 0)),
            pl.BlockSpec((1, d), lambda i: (0, 0)),
            _mod_spec(layer, 1, tpb, d, None),
            _mod_spec(layer, 0, tpb, d, None),
            pl.BlockSpec((6, d), lambda i: (0, 0)),
            pl.BlockSpec((d, hl), lambda i: (0, 0)),
        ],
        out_specs=[pl.BlockSpec((3, bm, d), lambda i: (0, i, 0)),
                   pl.BlockSpec((bm, hl), lambda i: (i, 0))],
        out_shape=[jax.ShapeDtypeStruct((3, n, d), BF16),
                   jax.ShapeDtypeStruct((n, hl), BF16)],
        compiler_params=_cparams(("parallel",)),
        name="rwkv_mix",
    )(x, x, g.reshape(1, d), mod, mod, mu, wd)


def _conv_in_kernel(h_ref, wb_ref, wc_ref, wu_ref, cw_ref, o_ref, carry_ref, w_ref, *, tpb):
    m = pl.program_id(1)

    @pl.when(m == 0)
    def _():
        w_ref[0] = wb_ref[...].astype(BF16)
        w_ref[1] = wc_ref[...].astype(BF16)
        w_ref[2] = wu_ref[...].astype(BF16)

    h = h_ref[...]
    bg = _dot(h, w_ref[0])
    z = _dot(h, w_ref[1]) * _dot(h, w_ref[2])
    bm = z.shape[0]
    prev = jnp.where(m % tpb == 0, 0.0, carry_ref[...])
    carry_ref[...] = z[bm - SUBLANE:bm]
    w0 = cw_ref[0:1, :]
    w1 = cw_ref[1:2, :]
    w2 = cw_ref[2:3, :]
    y = w0 * pltpu.roll(z, 2, 0) + w1 * pltpu.roll(z, 1, 0) + w2 * z
    o_ref[...] = (bg * y).astype(o_ref.dtype)
    zt = z[0:SUBLANE]
    r8 = lax.broadcasted_iota(jnp.int32, zt.shape, 0)
    z1 = jnp.where(r8 < 1, pltpu.roll(prev, 1, 0), pltpu.roll(zt, 1, 0))
    z2 = jnp.where(r8 < 2, pltpu.roll(prev, 2, 0), pltpu.roll(zt, 2, 0))
    yt = w0 * z2 + w1 * z1 + w2 * zt
    o_ref[0:SUBLANE, :] = (bg[0:SUBLANE] * yt).astype(o_ref.dtype)


def _conv_in(h, w_in, conv_w, layer, seq):
    n, d = h.shape
    bm = _blk(seq, 1024, SUBLANE)
    bn = _blk(d, 256, LANE)
    tpb = seq // bm
    nj = d // bn
    return pl.pallas_call(
        functools.partial(_conv_in_kernel, tpb=tpb),
        grid=(nj, n // bm),
        in_specs=[
            pl.BlockSpec((bm, d), lambda j, i: (i, 0)),
            pl.BlockSpec((None, d, bn), lambda j, i: (layer, 0, j)),
            pl.BlockSpec((None, d, bn), lambda j, i: (layer, 0, j + nj)),
            pl.BlockSpec((None, d, bn), lambda j, i: (layer, 0, j + 2 * nj)),
            pl.BlockSpec((None, CONV_WIDTH, bn), lambda j, i: (layer, 0, j)),
        ],
        out_specs=pl.BlockSpec((bm, bn), lambda j, i: (i, j)),
        out_shape=jax.ShapeDtypeStruct((n, d), BF16),
        scratch_shapes=[pltpu.VMEM((SUBLANE, bn), F32), pltpu.VMEM((3, d, bn), BF16)],
        compiler_params=_cparams(("arbitrary", "arbitrary")),
        name="conv_in",
    )(h, w_in, w_in, w_in, conv_w)


def _mm_resid_kernel(a_ref, w_ref, x_ref, gate_ref, o_ref, *, nk):
    if nk == 1:
        o_ref[...] = x_ref[...] + gate_ref[...] * _dot(a_ref[...], w_ref[...])
        return
    k = pl.program_id(2)

    @pl.when(k == 0)
    def _():
        o_ref[...] = jnp.zeros_like(o_ref)

    acc = o_ref[...] + _dot(a_ref[...], w_ref[...])
    o_ref[...] = jnp.where(k == nk - 1, x_ref[...] + gate_ref[...] * acc, acc)


def _mm_resid(a, w, widx, x, mod, layer, gate_chunk, seq):
    n, kdim = a.shape
    d = w.shape[2]
    bm = _blk(seq, 1024, SUBLANE)
    bk = _blk(kdim, 4096 if kdim <= 4096 else 2048, LANE)
    nk = kdim // bk
    bn = _blk(d, 512 if nk == 1 else 1024, LANE)
    tpb = seq // bm
    return pl.pallas_call(
        functools.partial(_mm_resid_kernel, nk=nk),
        grid=(n // bm, d // bn, nk),
        in_specs=[
            pl.BlockSpec((bm, bk), lambda i, j, k: (i, k)),
            pl.BlockSpec((None, bk, bn), lambda i, j, k: (widx, k, j)),
            pl.BlockSpec((bm, bn), lambda i, j, k: (i, j)),
            _mod_spec(layer, gate_chunk, tpb, bn, 1),
        ],
        out_specs=pl.BlockSpec((bm, bn), lambda i, j, k: (i, j)),
        out_shape=jax.ShapeDtypeStruct((n, d), F32),
        compiler_params=_cparams(("parallel", "parallel", "arbitrary")),
        name="mm_resid",
    )(a, w, x, mod)


def _mlp_up_kernel(a_ref, w_ref, o_ref):
    o_ref[...] = jnp.square(jnp.maximum(_dot(a_ref[...], w_ref[...]), 0.0)).astype(o_ref.dtype)


def _mlp_up(a, w, widx):
    n, kdim = a.shape
    f = w.shape[2]
    bm = _blk(n, 1024, SUBLANE)
    bn = _blk(f, 1024, LANE)
    return pl.pallas_call(
        _mlp_up_kernel,
        grid=(n // bm, f // bn),
        in_specs=[pl.BlockSpec((bm, kdim), lambda i, j: (i, 0)),
                  pl.BlockSpec((None, kdim, bn), lambda i, j: (widx, 0, j))],
        out_specs=pl.BlockSpec((bm, bn), lambda i, j: (i, j)),
        out_shape=jax.ShapeDtypeStruct((n, f), BF16),
        compiler_params=_cparams(("parallel", "parallel")),
        name="mlp_up",
    )(a, w)


def _rkv_kernel(a_ref, w_ref, o_ref, wb_ref):
    @pl.when(pl.program_id(2) == 0)
    def _():
        wb_ref[...] = w_ref[...].astype(BF16)

    o_ref[...] = _dot(a_ref[...], wb_ref[...])


def _rkv(mixes, w_rkv, widx):
    _, n, d = mixes.shape
    bm = _blk(n, 1024, SUBLANE)
    bn = _blk(d, 512, LANE)
    return pl.pallas_call(
        _rkv_kernel,
        grid=(3, d // bn, n // bm),
        in_specs=[pl.BlockSpec((None, bm, d), lambda s, j, i: (s, i, 0)),
                  pl.BlockSpec((None, None, d, bn), lambda s, j, i: (widx, s, 0, j))],
        out_specs=pl.BlockSpec((None, bm, bn), lambda s, j, i: (s, i, j)),
        out_shape=jax.ShapeDtypeStruct((3, n, d), F32),
        scratch_shapes=[pltpu.VMEM((d, bn), BF16)],
        compiler_params=_cparams(("arbitrary", "arbitrary", "arbitrary")),
        name="rkv_proj",
    )(mixes, w_rkv)


def _prep_kernel(*refs, widths, use_v):
    if use_v:
        (hid_ref, wu_ref, k_ref, v_ref, vf_ref, w0_ref, a0_ref, kk_ref, ka_ref, v0_ref,
         lw_ref, a_ref, g_ref, kkn_ref, km_ref, vm_ref) = refs
    else:
        (hid_ref, wu_ref, k_ref, w0_ref, a0_ref, kk_ref, ka_ref,
         lw_ref, a_ref, g_ref, kkn_ref, km_ref) = refs
    wl, al, gl, vl = widths
    o = 0
    x = w0_ref[...] + _dot(hid_ref[:, o:o + wl], wu_ref[o:o + wl, :])
    o += wl
    sp = jnp.maximum(-x, 0.0) + jnp.log(1.0 + jnp.exp(-jnp.abs(x)))
    lw_ref[...] = -jnp.exp(-sp - 0.5)
    a = jax.nn.sigmoid(a0_ref[...] + _dot(hid_ref[:, o:o + al], wu_ref[o:o + al, :]))
    o += al
    a_ref[...] = a
    g_ref[...] = _dot(hid_ref[:, o:o + gl], wu_ref[o:o + gl, :])
    o += gl
    k = k_ref[...]
    kk = k * kk_ref[...]
    ss = _segsum(kk * kk, _head_ones(LANE))
    kkn_ref[...] = kk * lax.rsqrt(jnp.maximum(ss, 1e-24))
    km_ref[...] = k * (1.0 + (a - 1.0) * ka_ref[...])
    if use_v:
        v = v_ref[...]
        vg = jax.nn.sigmoid(v0_ref[...] + _dot(hid_ref[:, o:o + vl], wu_ref[o:o + vl, :]))
        vm_ref[...] = v + (vf_ref[...] - v) * vg


def _prep(hid, wu, rkv, v_first, w0, a0, k_k, k_a, v0, widths, use_v):
    n, hl = hid.shape
    d = wu.shape[1]
    bm = _blk(n, 512, SUBLANE)
    bn = _blk(d, 1024, LANE)
    row = lambda p: p.reshape(1, d)
    blk = pl.BlockSpec((bm, bn), lambda i, j: (i, j))
    prow = pl.BlockSpec((1, bn), lambda i, j: (0, j))
    ins = [hid, wu, rkv]
    specs = [pl.BlockSpec((bm, hl), lambda i, j: (i, 0)),
             pl.BlockSpec((hl, bn), lambda i, j: (0, j)),
             pl.BlockSpec((None, bm, bn), lambda i, j: (1, i, j))]
    if use_v:
        vf_arr, vf_lead = v_first
        ins += [rkv, vf_arr]
        specs += [pl.BlockSpec((None, bm, bn), lambda i, j: (2, i, j)),
                  pl.BlockSpec((None, bm, bn), lambda i, j: (vf_lead, i, j))]
    ins += [row(w0), row(a0), row(k_k), row(k_a)]
    specs += [prow] * 4
    n_out = 5
    if use_v:
        ins.append(row(v0))
        specs.append(prow)
        n_out = 6
    return pl.pallas_call(
        functools.partial(_prep_kernel, widths=widths, use_v=use_v),
        grid=(n // bm, d // bn),
        in_specs=specs,
        out_specs=[blk] * n_out,
        out_shape=[jax.ShapeDtypeStruct((n, d), F32)] * n_out,
        compiler_params=_cparams(("parallel", "parallel")),
        name="rwkv_prep",
    )(*ins)


def _wkv_chunk(s_prev, ins, consts):
    tri, head_masks, strict_w, incl_w, blk_w, same, eye = consts
    ng = range(len(ins))

    def stack(x):
        return jnp.concatenate([jnp.where(m, x, 0.0) for m in head_masks], axis=0).astype(BF16)

    def bdot(x, y):
        return _dot(x, y).astype(BF16)

    ar, kb_s, v_s, vk_l, vk_r, gl = [], [], [], [], [], []
    for r, lw, k, v, kk, a in ins:
        h1, h2, h3 = _split3(lw)
        cum = _dot(tri, h1) + _dot(tri, h2) + _dot(tri, h3)
        cl = cum[CHUNK - 1:CHUNK]
        gi = jnp.exp(-cum)
        b = kk * a
        dl = jnp.exp(cl - cum)
        ar.append(jnp.concatenate([-(kk * jnp.exp(cum - lw)), r * jnp.exp(cum)], axis=0).astype(BF16))
        kb_s.append(jnp.concatenate([stack(k * gi), stack(b * gi)], axis=0))
        v_s.append(stack(v))
        vk_l.append(v.astype(BF16))
        vk_r.append(jnp.concatenate([k * dl, b * dl], axis=0).astype(BF16))
        gl.append(jnp.exp(cl))

    sc = [_dot_nt(ar[g], kb_s[g]) for g in ng]
    a_ak = [jnp.where(strict_w, sc[g][0:CHUNK, 0:GROUP], 0.0).astype(BF16) for g in ng]
    a_rk = [jnp.where(incl_w, sc[g][CHUNK:2 * CHUNK, 0:GROUP], 0.0).astype(BF16) for g in ng]
    a_rb = [jnp.where(incl_w, sc[g][CHUNK:2 * CHUNK, GROUP:2 * GROUP], 0.0).astype(BF16) for g in ng]
    n_w = [jnp.where(strict_w, sc[g][0:CHUNK, GROUP:2 * GROUP], 0.0) for g in ng]
    dm = [stack(jnp.where(blk_w, n_w[g], 0.0)) for g in ng]
    e = [stack(jnp.where(blk_w, 0.0, n_w[g])) for g in ng]

    ss = [_dot_nt(ar[g], s_prev[g].astype(BF16)) for g in ng]
    pm = [stack(ss[g][0:CHUNK] + _dot(a_ak[g], v_s[g])) for g in ng]
    y0 = [ss[g][CHUNK:2 * CHUNK] + _dot(a_rk[g], v_s[g]) for g in ng]

    p = dm
    td = [eye + dm[g] for g in ng]
    for _ in range(3):
        p = [bdot(p[g], p[g]) for g in ng]
        td = [(td[g].astype(F32) + _dot(td[g], p[g])).astype(BF16) for g in ng]
    m1 = [bdot(td[g], e[g]) for g in ng]
    u = [bdot(td[g], pm[g]) for g in ng]
    m2 = [bdot(m1[g], m1[g]) for g in ng]
    w = [(u[g].astype(F32) + _dot(m2[g], u[g])).astype(BF16) for g in ng]
    sa = [(w[g].astype(F32) + _dot(m1[g], w[g])).astype(BF16) for g in ng]

    y, s_next = [], []
    for g in ng:
        y.append(y0[g] + _dot(a_rb[g], sa[g]))
    for g in ng:
        sa_n = sa[g][0:CHUNK]
        for hh in range(1, GROUP_HEADS):
            sa_n = sa_n + sa[g][hh * CHUNK:(hh + 1) * CHUNK]
        lhs = jnp.concatenate([vk_l[g], sa_n], axis=0)
        upd = _dot_tn(lhs, vk_r[g])
        s_next.append(s_prev[g] * gl[g] + jnp.where(same, upd, 0.0))
    return y, s_next


def _wkv_kernel(r_ref, lw_ref, k_ref, v_ref, kk_ref, a_ref, y_ref, s_ref, *, groups):
    c = pl.program_id(2)

    @pl.when(c == 0)
    def _():
        s_ref[...] = jnp.zeros_like(s_ref)

    ri = lax.broadcasted_iota(jnp.int32, (GROUP, GROUP), 0)
    ci = lax.broadcasted_iota(jnp.int32, (GROUP, GROUP), 1)
    same = (ri // CHUNK) == (ci // CHUNK)
    eye = jnp.where(ri == ci, 1.0, 0.0).astype(BF16)
    wt = lax.broadcasted_iota(jnp.int32, (CHUNK, GROUP), 0)
    ws = lax.broadcasted_iota(jnp.int32, (CHUNK, GROUP), 1) % CHUNK
    strict_w = wt > ws
    incl_w = wt >= ws
    blk_w = (wt // SUB) == (ws // SUB)
    ti = lax.broadcasted_iota(jnp.int32, (CHUNK, CHUNK), 0)
    tj = lax.broadcasted_iota(jnp.int32, (CHUNK, CHUNK), 1)
    tri = jnp.where(ti >= tj, 1.0, 0.0).astype(BF16)
    lane_head = lax.broadcasted_iota(jnp.int32, (CHUNK, GROUP), 1) // HEAD
    head_masks = [lane_head == hh for hh in range(GROUP_HEADS)]
    consts = (tri, head_masks, strict_w, incl_w, blk_w, same, eye)

    cols = [slice(g * GROUP, (g + 1) * GROUP) for g in range(groups)]
    ins = [tuple(ref[:, sl] for ref in (r_ref, lw_ref, k_ref, v_ref, kk_ref, a_ref)) for sl in cols]
    y, s_next = _wkv_chunk([s_ref[g] for g in range(groups)], ins, consts)
    for g, sl in enumerate(cols):
        y_ref[:, sl] = y[g]
        s_ref[g] = s_next[g]


def _wkv(r_src, lw, k, v_src, kk, a, bsz, seq):
    n, d = lw.shape
    groups = _blk(d, WKV_GROUPS * GROUP, GROUP) // GROUP
    gw = groups * GROUP
    nc = seq // CHUNK

    def spec(src):
        arr, lead = src
        if lead is None:
            return arr, pl.BlockSpec((CHUNK, gw), lambda bb, j, c: (bb * nc + c, j))
        return arr, pl.BlockSpec((None, CHUNK, gw), lambda bb, j, c: (lead, bb * nc + c, j))

    ins, specs = zip(*[spec(s) for s in (r_src, (lw, None), (k, None), v_src, (kk, None), (a, None))])
    return pl.pallas_call(
        functools.partial(_wkv_kernel, groups=groups),
        grid=(bsz, d // gw, nc),
        in_specs=list(specs),
        out_specs=pl.BlockSpec((CHUNK, gw), lambda bb, j, c: (bb * nc + c, j)),
        out_shape=jax.ShapeDtypeStruct((n, d), F32),
        scratch_shapes=[pltpu.VMEM((groups, GROUP, GROUP), F32)],
        compiler_params=_cparams(("parallel", "parallel", "arbitrary")),
        name="wkv7",
    )(*ins)


def _post_kernel(y_ref, r_ref, k_ref, v_ref, g_ref, lg_ref, lb_ref, rk_ref, o_ref):
    ones = _head_ones(LANE)
    y = y_ref[...]
    inv = 1.0 / HEAD
    dlt = y - _segsum(y, ones) * inv
    var = _segsum(dlt * dlt, ones) * inv
    yn = dlt * lax.rsqrt(var + GN_EPS) * lg_ref[...] + lb_ref[...]
    bonus = _segsum(r_ref[...] * k_ref[...] * rk_ref[...], ones) * v_ref[...]
    o_ref[...] = ((yn + bonus) * g_ref[...]).astype(o_ref.dtype)


def _post(y, rkv, km, v_src, g, lnx_g, lnx_b, r_k):
    n, d = y.shape
    bm = _blk(n, 512, SUBLANE)
    bn = _blk(d, 1024, LANE)
    blk = pl.BlockSpec((bm, bn), lambda i, j: (i, j))
    prow = pl.BlockSpec((1, bn), lambda i, j: (0, j))
    v_arr, v_lead = v_src
    v_spec = blk if v_lead is None else pl.BlockSpec((None, bm, bn), lambda i, j: (v_lead, i, j))
    return pl.pallas_call(
        _post_kernel,
        grid=(n // bm, d // bn),
        in_specs=[blk, pl.BlockSpec((None, bm, bn), lambda i, j: (0, i, j)), blk, v_spec, blk,
                  prow, prow, prow],
        out_specs=blk,
        out_shape=jax.ShapeDtypeStruct((n, d), BF16),
        compiler_params=_cparams(("parallel", "parallel")),
        name="rwkv_post",
    )(y, rkv, km, v_arr, g, lnx_g.reshape(1, d), lnx_b.reshape(1, d), r_k.reshape(1, d))


def _pad_to(w, axis, mult):
    size = w.shape[axis]
    target = -(-size // mult) * mult
    if target == size:
        return w
    pad = [(0, 0)] * w.ndim
    pad[axis] = (0, target - size)
    return jnp.pad(w, pad)


def kernel(x, c, ada_w, ada_b, ada_emb, ln1_g, ln2_g, mlp_w1, mlp_w2, conv_w_in, conv_w, conv_w_out, rwkv_mu, rwkv_w_rkv, rwkv_w0, rwkv_w1, rwkv_w2, rwkv_a0, rwkv_a1, rwkv_a2, rwkv_g1, rwkv_g2, rwkv_k_k, rwkv_k_a, rwkv_r_k, rwkv_lnx_g, rwkv_lnx_b, rwkv_w_o, rwkv_v0, rwkv_v1, rwkv_v2, final_g):
    bsz, seq, d = x.shape
    depth = ada_emb.shape[0]
    n = bsz * seq
    assert d % GROUP == 0 and seq % CHUNK == 0

    mod = _ada(c, ada_w, ada_b, ada_emb)
    w1_b, w2_b = mlp_w1.astype(BF16), mlp_w2.astype(BF16)
    cout_b = conv_w_out.astype(BF16)
    wo_b = rwkv_w_o.astype(BF16)
    xs = x.reshape(n, d)
    v_first = None
    for i in range(depth):
        j = i // 2
        if i % 2 == 0:
            h = _norm_mod(xs, ln1_g[i], mod, i, 1, 0, seq)
            gated = _conv_in(h, conv_w_in, conv_w, j, seq)
            xs = _mm_resid(gated, cout_b, j, xs, mod, i, 2, seq)
        else:
            use_v = v_first is not None
            downs = [rwkv_w1[j], rwkv_a1[j], rwkv_g1[j]]
            ups = [rwkv_w2[j], rwkv_a2[j], rwkv_g2[j]]
            if use_v:
                downs.append(rwkv_v1[j - 1])
                ups.append(rwkv_v2[j - 1])
            downs = [_pad_to(w, 1, LANE) for w in downs]
            ups = [_pad_to(w, 0, LANE) for w in ups]
            widths = tuple(w.shape[1] for w in downs) + ((0,) if not use_v else ())
            wd = jnp.concatenate(downs, axis=1).astype(BF16)
            wu = jnp.concatenate(ups, axis=0).astype(BF16)
            mixes, hid = _mix(xs, ln1_g[i], mod, i, rwkv_mu[j], wd, widths, use_v, seq)
            rkv = _rkv(mixes, rwkv_w_rkv, j)
            outs = _prep(hid, wu, rkv, v_first, rwkv_w0[j], rwkv_a0[j], rwkv_k_k[j],
                         rwkv_k_a[j], rwkv_v0[j - 1] if use_v else None, widths, use_v)
            if use_v:
                lw, a, g, kkn, km, vm = outs
                v_src = (vm, None)
            else:
                lw, a, g, kkn, km = outs
                v_src = (rkv, 2)
                v_first = (rkv, 2)
            y = _wkv((rkv, 0), lw, km, v_src, kkn, a, bsz, seq)
            gated = _post(y, rkv, km, v_src, g, rwkv_lnx_g[j], rwkv_lnx_b[j], rwkv_r_k[j])
            xs = _mm_resid(gated, wo_b, j, xs, mod, i, 2, seq)
        h = _norm_mod(xs, ln2_g[i], mod, i, 4, 3, seq)
        hidden = _mlp_up(h, w1_b, i)
        xs = _mm_resid(hidden, w2_b, i, xs, mod, i, 5, seq)
    return _final_norm(xs, final_g).reshape(bsz, seq, d)
```
